```python
import math
import jax
import jax.numpy as jnp
from jax import lax
import numpy as np

D_MODEL = 1024
BATCH = 4
SEQ = 4096
DEPTH = 2
DEC_BATCH = 128
DEC_SEQ = 1
PAST_LEN = 8192
PAGE_SIZE = 128

GDN_HEADS = 4
GDN_DK = 128
GDN_DV = 128
GDN_CHUNK = 64
CONV_W = 4
LRU_WIDTH = 512
LRU_BLOCKS = 8
LRU_BW = LRU_WIDTH // LRU_BLOCKS
LRU_C = 8.0
SWA_HEADS = 16
SWA_KV_HEADS = 4
SWA_HEAD_DIM = 64
SWA_GROUP = SWA_HEADS // SWA_KV_HEADS
WINDOW = 128
REL_BUCKETS = 32
REL_MAX_DIST = 128
PEER_HEADS = 8
PEER_N_KEYS = 128
PEER_N_EXPERTS = PEER_N_KEYS ** 2
PEER_D_QUERY = 256
PEER_D_SUB = PEER_D_QUERY // 2
PEER_TOPK = 16
PEER_BLOCK = 128
DN_ALPHA = (2 * DEPTH) ** 0.25
DN_BETA = (8 * DEPTH) ** -0.25
LN_EPS = 1e-5
N_AB = (DEPTH + 1) // 2
N_C = DEPTH // 2
GDN_QK_W = GDN_HEADS * GDN_DK
GDN_V_W = GDN_HEADS * GDN_DV
GDN_CONV_CH = 2 * GDN_QK_W + GDN_V_W
AB_IN_COLS = GDN_CONV_CH + GDN_V_W + 2 * GDN_HEADS + 2 * LRU_WIDTH
AB_MIX_W = GDN_V_W + LRU_WIDTH
SWA_Q_W = SWA_HEADS * SWA_HEAD_DIM
SWA_KV_W = SWA_KV_HEADS * SWA_HEAD_DIM
C_IN_COLS = SWA_Q_W + 2 * SWA_KV_W

kernel_name = 'hybrid_gdn_rglru_swa_peer_step'

F32 = jnp.float32


def layer_norm(x, g, b):
    xf = x.astype(F32)
    mu = xf.mean(-1, keepdims=True)
    var = jnp.square(xf - mu).mean(-1, keepdims=True)
    return ((xf - mu) * lax.rsqrt(var + LN_EPS) * g + b).astype(x.dtype)


def causal_conv(x, w, buf, bias=None):
    xp = jnp.concatenate([buf.astype(x.dtype), x], axis=1)
    t_len = x.shape[1]
    y = sum(xp[:, i:i + t_len] * w[i] for i in range(CONV_W))
    if bias is not None:
        y = y + bias
    return y, xp[:, -(CONV_W - 1):]


def l2_normalize(x):
    return x * lax.rsqrt(jnp.sum(x * x, axis=-1, keepdims=True) + 1e-6)


def gated_delta_rule(q, k, v, g, beta, s0):
    bsz, t_len, n_h, _ = q.shape
    n_chunks = -(-t_len // GDN_CHUNK)
    pad = n_chunks * GDN_CHUNK - t_len

    def chunks(t):
        t = jnp.pad(t, [(0, 0), (0, pad)] + [(0, 0)] * (t.ndim - 2))
        t = t.reshape((bsz, n_chunks, GDN_CHUNK) + t.shape[2:])
        return jnp.moveaxis(t, 3, 1)

    q, k, v, g, beta = (chunks(t) for t in (q, k, v, g, beta))
    idx = jnp.arange(GDN_CHUNK)
    incl = idx[:, None] >= idx[None, :]
    strict = idx[:, None] > idx[None, :]
    gc = jnp.cumsum(g, axis=-1)
    decay = jnp.exp(jnp.where(incl, gc[..., :, None] - gc[..., None, :], -jnp.inf))
    kb = k * beta[..., None]
    lower = jnp.where(strict, jnp.einsum('bhncd,bhnsd->bhncs', kb, k) * decay, 0.0)
    rhs = jnp.concatenate([v * beta[..., None], kb * jnp.exp(gc)[..., None]], axis=-1)
    sol = lax.linalg.triangular_solve(lower, rhs, left_side=True, lower=True, unit_diagonal=True)
    u, w = sol[..., :GDN_DV], sol[..., GDN_DV:]
    qk = jnp.einsum('bhncd,bhnsd->bhncs', q, k) * decay

    def step(s, inp):
        q_c, k_c, u_c, w_c, gc_c, qk_c = inp
        v_new = u_c - jnp.einsum('bhcd,bhde->bhce', w_c, s)
        o = (jnp.einsum('bhcd,bhde->bhce', q_c * jnp.exp(gc_c)[..., None], s)
             + jnp.einsum('bhcs,bhse->bhce', qk_c, v_new))
        g_last = gc_c[..., -1:]
        s = (s * jnp.exp(g_last)[..., None]
             + jnp.einsum('bhcd,bhce->bhde', k_c * jnp.exp(g_last - gc_c)[..., None], v_new))
        return s, o

    xs = tuple(jnp.moveaxis(t, 2, 0) for t in (q, k, u, w, gc, qk))
    s_last, o = lax.scan(step, s0, xs)
    o = jnp.moveaxis(o, 0, 2).reshape(bsz, n_h, n_chunks * GDN_CHUNK, GDN_DV)[:, :, :t_len]
    return jnp.moveaxis(o, 1, 2), s_last


def _lru_combine(e1, e2):
    return e1[0] * e2[0], e2[0] * e1[1] + e2[1]


def rg_lru(x, h0, w_r, b_r, w_i, b_i, lam):
    bsz, t_len, _ = x.shape
    xf = x.astype(F32)
    xb = xf.reshape(bsz, t_len, LRU_BLOCKS, LRU_BW)
    r = jax.nn.sigmoid(jnp.einsum('btnc,ncd->btnd', xb, w_r).reshape(bsz, t_len, LRU_WIDTH) + b_r)
    i = jax.nn.sigmoid(jnp.einsum('btnc,ncd->btnd', xb, w_i).reshape(bsz, t_len, LRU_WIDTH) + b_i)
    log_a = -LRU_C * r * jax.nn.softplus(-lam.astype(F32))
    a = jnp.exp(log_a)
    b = jnp.sqrt(-jnp.expm1(2.0 * log_a)) * (i * xf)
    b = b.at[:, 0].add(a[:, 0] * h0.astype(F32))
    _, h = lax.associative_scan(_lru_combine, (a, b), axis=1)
    return h, h[:, -1]


def ab_mixer(x, s0, gdn_buf, h0, lru_buf, w_in, gdn_conv_w, gdn_a_log, gdn_dt_bias, gdn_norm_w,
             lru_conv_w, lru_conv_b, lru_w_r, lru_b_r, lru_w_i, lru_b_i, lru_lam, w_out):
    bsz, t_len, _ = x.shape
    proj = x @ w_in
    splits = np.cumsum([GDN_CONV_CH, GDN_V_W, GDN_HEADS, GDN_HEADS, LRU_WIDTH]).tolist()
    qkv, z, a_raw, b_raw, xr, gate = jnp.split(proj, splits, axis=-1)
    qkv, gdn_buf_new = causal_conv(qkv, gdn_conv_w, gdn_buf)
    qkv = jax.nn.silu(qkv).astype(F32)
    q, k, v = jnp.split(qkv, [GDN_QK_W, 2 * GDN_QK_W], axis=-1)
    q = l2_normalize(q.reshape(bsz, t_len, GDN_HEADS, GDN_DK)) * (GDN_DK ** -0.5)
    k = l2_normalize(k.reshape(bsz, t_len, GDN_HEADS, GDN_DK))
    v = v.reshape(bsz, t_len, GDN_HEADS, GDN_DV)
    beta = jax.nn.sigmoid(b_raw.astype(F32))
    g = -jnp.exp(gdn_a_log.astype(F32)) * jax.nn.softplus(a_raw.astype(F32) + gdn_dt_bias)
    o, s_new = gated_delta_rule(q, k, v, g, beta, s0.astype(F32))
    o = o * lax.rsqrt(jnp.mean(o * o, axis=-1, keepdims=True) + 1e-6) * gdn_norm_w
    o = o * jax.nn.silu(z.astype(F32).reshape(bsz, t_len, GDN_HEADS, GDN_DV))
    o_a = o.reshape(bsz, t_len, GDN_V_W).astype(x.dtype)
    xr, lru_buf_new = causal_conv(xr, lru_conv_w, lru_buf, lru_conv_b)
    h, h_last = rg_lru(xr, h0, lru_w_r, lru_b_r, lru_w_i, lru_b_i, lru_lam)
    o_b = (jax.nn.gelu(gate.astype(F32)) * h).astype(x.dtype)
    y = jnp.concatenate([o_a, o_b], axis=-1) @ w_out
    return y, s_new, gdn_buf_new, h_last, lru_buf_new


def t5_bucket(rel):
    n = jnp.maximum(rel, 0)
    exact = REL_BUCKETS // 2
    nf = jnp.maximum(n, 1).astype(F32)
    large = exact + (jnp.log(nf / exact) / math.log(REL_MAX_DIST / exact)
                     * (REL_BUCKETS - exact)).astype(jnp.int32)
    return jnp.where(n < exact, n, jnp.minimum(large, REL_BUCKETS - 1))


def swa_project(x, w_in, b_in):
    bsz, t_len, _ = x.shape
    proj = x @ w_in + b_in
    q, k, v = jnp.split(proj, [SWA_Q_W, SWA_Q_W + SWA_KV_W], axis=-1)
    return (q.reshape(bsz, t_len, SWA_KV_HEADS, SWA_GROUP, SWA_HEAD_DIM),
            k.reshape(bsz, t_len, SWA_KV_HEADS, SWA_HEAD_DIM),
            v.reshape(bsz, t_len, SWA_KV_HEADS, SWA_HEAD_DIM))


def banded_attention(q, k, v, qpos, kpos, rel_bias, sinks):
    n_blk, n_q = qpos.shape
    n_k = kpos.shape[1]
    rel = qpos[:, :, None] - kpos[:, None, :]
    mask = (rel >= 0) & (rel < WINDOW) & (kpos[:, None, :] >= 0)
    bias = rel_bias.astype(F32)[t5_bucket(rel)]
    bias = bias.reshape(n_blk, n_q, n_k, SWA_KV_HEADS, SWA_GROUP).transpose(0, 3, 4, 1, 2)
    logits = (jnp.einsum('bnqhgd,bnkhd->bnhgqk', q.astype(F32), k.astype(F32)) * (SWA_HEAD_DIM ** -0.5)
              + bias[None])
    logits = jnp.where(mask[None, :, None, None], logits, -jnp.inf)
    sink = sinks.astype(F32).reshape(SWA_KV_HEADS, SWA_GROUP)[None, None, :, :, None, None]
    m = jnp.maximum(logits.max(-1, keepdims=True), sink)
    p = jnp.exp(logits - m)
    probs = p / (p.sum(-1, keepdims=True) + jnp.exp(sink - m))
    return jnp.einsum('bnhgqk,bnkhd->bnqhgd', probs, v.astype(F32))


def swa_prompt(x, w_in, b_in, sinks, w_out, b_out, rel_bias):
    bsz, t_len, _ = x.shape
    q, k, v = swa_project(x, w_in, b_in)
    n_blk = t_len // WINDOW

    def key_blocks(t):
        tp = jnp.concatenate([jnp.zeros_like(t[:, :WINDOW]), t], axis=1)
        prev = tp[:, :t_len].reshape((bsz, n_blk, WINDOW) + t.shape[2:])
        cur = tp[:, WINDOW:].reshape((bsz, n_blk, WINDOW) + t.shape[2:])
        return jnp.concatenate([prev, cur], axis=2)

    base = jnp.arange(n_blk)[:, None] * WINDOW
    qpos = base + jnp.arange(WINDOW)[None, :]
    kpos = base - WINDOW + jnp.arange(2 * WINDOW)[None, :]
    qb = q.reshape(bsz, n_blk, WINDOW, SWA_KV_HEADS, SWA_GROUP, SWA_HEAD_DIM)
    o = banded_attention(qb, key_blocks(k), key_blocks(v), qpos, kpos, rel_bias, sinks)
    y = o.reshape(bsz, t_len, SWA_Q_W).astype(x.dtype) @ w_out + b_out
    return y, k[:, -WINDOW:], v[:, -WINDOW:]


def swa_sample(x, k_buf, v_buf, w_in, b_in, sinks, w_out, b_out, rel_bias):
    bsz, t_len, _ = x.shape
    q, k, v = swa_project(x, w_in, b_in)
    wb = k_buf.shape[1]
    kc = jnp.concatenate([k_buf.astype(k.dtype), k], axis=1)
    vc = jnp.concatenate([v_buf.astype(v.dtype), v], axis=1)
    qpos = (PAST_LEN + jnp.arange(t_len))[None, :]
    kpos = (PAST_LEN - wb + jnp.arange(wb + t_len))[None, :]
    o = banded_attention(q[:, None], kc[:, None], vc[:, None], qpos, kpos, rel_bias, sinks)[:, 0]
    y = o.reshape(bsz, t_len, SWA_Q_W).astype(x.dtype) @ w_out + b_out
    return y, kc[:, -wb:], vc[:, -wb:]


def peer(x, w_q, sub_keys, u_tab, v_tab):
    shp = x.shape
    t = x.reshape(-1, D_MODEL)
    n_tok = t.shape[0]
    n_blocks = -(-n_tok // PEER_BLOCK)
    t = jnp.pad(t, ((0, n_blocks * PEER_BLOCK - n_tok), (0, 0)))

    def block(tb):
        q = (tb @ w_q).reshape(PEER_BLOCK, PEER_HEADS, 2, PEER_D_SUB)
        s = jnp.einsum('thcd,hcnd->thcn', q, sub_keys).astype(F32)
        s_top, i_top = lax.top_k(s, PEER_TOPK)
        cand = (s_top[:, :, 0, :, None] + s_top[:, :, 1, None, :]).reshape(PEER_BLOCK, PEER_HEADS, -1)
        cand_idx = (i_top[:, :, 0, :, None] * PEER_N_KEYS + i_top[:, :, 1, None, :]).reshape(
            PEER_BLOCK, PEER_HEADS, -1)
        best, pos = lax.top_k(cand, PEER_TOPK)
        expert = jnp.take_along_axis(cand_idx, pos, axis=-1)
        gate = jax.nn.softmax(best, axis=-1)
        act = jax.nn.gelu(jnp.einsum('td,thkd->thk', tb, u_tab[expert]).astype(F32))
        return jnp.einsum('thk,thkd->td', (gate * act).astype(tb.dtype), v_tab[expert])

    y = lax.map(block, t.reshape(n_blocks, PEER_BLOCK, D_MODEL))
    return y.reshape(-1, D_MODEL)[:n_tok].reshape(shp)


def setup_inputs(seed: int = 0) -> dict:
    key = jax.random.key(seed)
    keys = iter(jax.random.split(key, 48))

    def nrm(shape, scale):
        return scale * jax.random.normal(next(keys), shape, jnp.float32)

    win_buf = min(WINDOW, PAST_LEN)
    a_init = jax.random.uniform(next(keys), (N_AB, LRU_WIDTH), jnp.float32, 0.9, 0.999)
    sig_l = a_init ** (1.0 / LRU_C)
    lru_lam = jnp.log(sig_l) - jnp.log1p(-sig_l)
    gdn_a_log = jnp.log(jax.random.uniform(next(keys), (N_AB, GDN_HEADS), jnp.float32, 1.0, 16.0))
    dt = jnp.exp(jax.random.uniform(next(keys), (N_AB, GDN_HEADS), jnp.float32,
                                    math.log(1e-3), math.log(1e-1)))
    gdn_dt_bias = dt + jnp.log(-jnp.expm1(-dt))
    return {
        'x_prompt': nrm((BATCH, SEQ, D_MODEL), 1.0),
        'x_sample': nrm((DEC_BATCH, DEC_SEQ, D_MODEL), 1.0),
        'state_gdn': nrm((N_AB, DEC_BATCH, GDN_HEADS, GDN_DK, GDN_DV), 0.05),
        'state_gdn_conv': nrm((N_AB, DEC_BATCH, CONV_W - 1, GDN_CONV_CH), 1.0),
        'state_lru': nrm((N_AB, DEC_BATCH, LRU_WIDTH), 0.5),
        'state_lru_conv': nrm((N_AB, DEC_BATCH, CONV_W - 1, LRU_WIDTH), 1.0),
        'cache_swa_k': nrm((N_C, DEC_BATCH, win_buf, SWA_KV_HEADS, SWA_HEAD_DIM), 1.0),
        'cache_swa_v': nrm((N_C, DEC_BATCH, win_buf, SWA_KV_HEADS, SWA_HEAD_DIM), 1.0),
        'w_in_ab': nrm((N_AB, D_MODEL, AB_IN_COLS), D_MODEL ** -0.5),
        'gdn_conv_w': nrm((N_AB, CONV_W, GDN_CONV_CH), CONV_W ** -0.5),
        'gdn_a_log': gdn_a_log,
        'gdn_dt_bias': gdn_dt_bias,
        'gdn_norm_w': 1.0 + nrm((N_AB, GDN_DV), 0.01),
        'lru_conv_w': nrm((N_AB, CONV_W, LRU_WIDTH), CONV_W ** -0.5),
        'lru_conv_b': nrm((N_AB, LRU_WIDTH), 0.01),
        'lru_w_r': nrm((N_AB, LRU_BLOCKS, LRU_BW, LRU_BW), LRU_BW ** -0.5),
        'lru_b_r': nrm((N_AB, LRU_WIDTH), 0.01),
        'lru_w_i': nrm((N_AB, LRU_BLOCKS, LRU_BW, LRU_BW), LRU_BW ** -0.5),
        'lru_b_i': nrm((N_AB, LRU_WIDTH), 0.01),
        'lru_lam': lru_lam,
        'w_out_ab': nrm((N_AB, AB_MIX_W, D_MODEL), DN_BETA * AB_MIX_W ** -0.5),
        'w_in_c': nrm((N_C, D_MODEL, C_IN_COLS), D_MODEL ** -0.5),
        'b_in_c': nrm((N_C, C_IN_COLS), 0.01),
        'swa_sinks': nrm((N_C, SWA_HEADS), 0.5),
        'w_out_c': nrm((N_C, SWA_Q_W, D_MODEL), DN_BETA * SWA_Q_W ** -0.5),
        'b_out_c': nrm((N_C, D_MODEL), 0.01),
        'rel_bias': nrm((REL_BUCKETS, SWA_HEADS), 0.3),
        'ln_mix_g': 1.0 + nrm((DEPTH, D_MODEL), 0.01),
        'ln_mix_b': nrm((DEPTH, D_MODEL), 0.01),
        'ln_ffn_g': 1.0 + nrm((DEPTH, D_MODEL), 0.01),
        'ln_ffn_b': nrm((DEPTH, D_MODEL), 0.01),
        'peer_w_q': nrm((DEPTH, D_MODEL, PEER_HEADS * PEER_D_QUERY), D_MODEL ** -0.5),
        'peer_keys': nrm((DEPTH, PEER_HEADS, 2, PEER_N_KEYS, PEER_D_SUB), PEER_D_SUB ** -0.5),
        'peer_u': nrm((DEPTH, PEER_N_EXPERTS, D_MODEL), D_MODEL ** -0.5),
        'peer_v': nrm((DEPTH, PEER_N_EXPERTS, D_MODEL), DN_BETA * PEER_HEADS ** -0.5),
    }


def reference(x_prompt, x_sample, state_gdn, state_gdn_conv, state_lru, state_lru_conv,
              cache_swa_k, cache_swa_v, w_in_ab, gdn_conv_w, gdn_a_log, gdn_dt_bias, gdn_norm_w,
              lru_conv_w, lru_conv_b, lru_w_r, lru_b_r, lru_w_i, lru_b_i, lru_lam, w_out_ab,
              w_in_c, b_in_c, swa_sinks, w_out_c, b_out_c, rel_bias,
              ln_mix_g, ln_mix_b, ln_ffn_g, ln_ffn_b, peer_w_q, peer_keys, peer_u, peer_v):
    xp, xs = x_prompt, x_sample
    bsz = xp.shape[0]
    p_gdn, p_gdn_conv, p_lru, p_lru_conv, p_k, p_v = [], [], [], [], [], []
    s_gdn, s_gdn_conv, s_lru, s_lru_conv, s_k, s_v = [], [], [], [], [], []
    for layer in range(DEPTH):
        j = layer // 2
        if layer % 2 == 0:
            ab_w = (w_in_ab[j], gdn_conv_w[j], gdn_a_log[j], gdn_dt_bias[j], gdn_norm_w[j],
                    lru_conv_w[j], lru_conv_b[j], lru_w_r[j], lru_b_r[j], lru_w_i[j], lru_b_i[j],
                    lru_lam[j], w_out_ab[j])
            mp, st, cb, hl, lb = ab_mixer(
                xp, jnp.zeros((bsz, GDN_HEADS, GDN_DK, GDN_DV), F32),
                jnp.zeros((bsz, CONV_W - 1, GDN_CONV_CH), xp.dtype),
                jnp.zeros((bsz, LRU_WIDTH), F32),
                jnp.zeros((bsz, CONV_W - 1, LRU_WIDTH), xp.dtype), *ab_w)
            p_gdn.append(st); p_gdn_conv.append(cb); p_lru.append(hl); p_lru_conv.append(lb)
            ms, st, cb, hl, lb = ab_mixer(xs, state_gdn[j], state_gdn_conv[j], state_lru[j],
                                          state_lru_conv[j], *ab_w)
            s_gdn.append(st); s_gdn_conv.append(cb); s_lru.append(hl); s_lru_conv.append(lb)
        else:
            c_w = (w_in_c[j], b_in_c[j], swa_sinks[j], w_out_c[j], b_out_c[j], rel_bias)
            mp, kk, vv = swa_prompt(xp, *c_w)
            p_k.append(kk); p_v.append(vv)
            ms, kk, vv = swa_sample(xs, cache_swa_k[j], cache_swa_v[j], *c_w)
            s_k.append(kk); s_v.append(vv)
        xp = layer_norm(DN_ALPHA * xp + mp, ln_mix_g[layer], ln_mix_b[layer])
        xs = layer_norm(DN_ALPHA * xs + ms, ln_mix_g[layer], ln_mix_b[layer])
        peer_w = (peer_w_q[layer], peer_keys[layer], peer_u[layer], peer_v[layer])
        xp = layer_norm(DN_ALPHA * xp + peer(xp, *peer_w), ln_ffn_g[layer], ln_ffn_b[layer])
        xs = layer_norm(DN_ALPHA * xs + peer(xs, *peer_w), ln_ffn_g[layer], ln_ffn_b[layer])
    return (xp, xs,
            jnp.stack(p_gdn), jnp.stack(p_gdn_conv), jnp.stack(p_lru), jnp.stack(p_lru_conv),
            jnp.stack(p_k), jnp.stack(p_v),
            jnp.stack(s_gdn), jnp.stack(s_gdn_conv), jnp.stack(s_lru), jnp.stack(s_lru_conv),
            jnp.stack(s_k), jnp.stack(s_v))
```

```python
import functools
import math

import jax
import jax.numpy as jnp
from jax import lax
from jax.experimental import pallas as pl
from jax.experimental.pallas import tpu as pltpu

F32 = jnp.float32
BF16 = jnp.bfloat16

D_MODEL = 1024
DEPTH = 2
DN_ALPHA = (2 * DEPTH) ** 0.25
LN_EPS = 1e-5

PEER_HEADS = 8
PEER_N_KEYS = 128
PEER_D_SUB = 128
PEER_TOPK = 16
PEER_Q_W = PEER_HEADS * 2 * PEER_D_SUB

LANES = 128
SUBLANES = 8
VMEM_LIMIT = 56 * 1024 * 1024

TOK_TILE = 256
EXP_BLOCK = 1024
NEG_INF = float("-inf")


def _oddeven_merge(lo, hi, r):
    step = r * 2
    if step < hi - lo:
        yield from _oddeven_merge(lo, hi, step)
        yield from _oddeven_merge(lo + r, hi, step)
        for i in range(lo + r, hi - r, step):
            yield (i, i + r)
    else:
        yield (lo, lo + r)


def _oddeven_sort_pairs(lo, hi):
    if hi - lo >= 1:
        mid = lo + (hi - lo) // 2
        yield from _oddeven_sort_pairs(lo, mid)
        yield from _oddeven_sort_pairs(mid + 1, hi)
        yield from _oddeven_merge(lo, hi, 1)


_SORT16 = tuple(_oddeven_sort_pairs(0, 15))


def _cmpx(v, i, j):
    a, b = v[i], v[j]
    if b is None:
        return
    if a is None:
        v[i], v[j] = b, None
        return
    v[i], v[j] = jnp.maximum(a, b), jnp.minimum(a, b)


def _sort16_desc(v):
    v = list(v)
    for i, j in _SORT16:
        _cmpx(v, i, j)
    return v


def _merge_top16(x, y):
    v = []
    for k in range(16):
        a, b = x[k], y[15 - k]
        v.append(b if a is None else a if b is None else jnp.maximum(a, b))
    for d in (8, 4, 2, 1):
        for i in range(16):
            if not i & d:
                _cmpx(v, i, i + d)
    return v


def _top16_of_keys(s_t):
    v = _sort16_desc([s_t[SUBLANES * r:SUBLANES * (r + 1), :] for r in range(16)])
    for shift in (4, 2, 1):
        v = _merge_top16(v, [pltpu.roll(a, shift, 0) for a in v])
    return v


_PAIR_ROWS = [[(k, l) for l in range(PEER_TOPK // (k + 1))] for k in range(PEER_TOPK)]


def _top16_pair_sums(a, b):
    top = [a[0] + b[l] for l in range(16)]
    rest = [a[k] + b[l] for row in _PAIR_ROWS[1:] for (k, l) in row]
    for lo in range(0, len(rest), 16):
        chunk = rest[lo:lo + 16]
        chunk = chunk + [None] * (16 - len(chunk))
        top = _merge_top16(top, _sort16_desc(chunk))
    return top


def _peer_select_kernel(x_ref, wq_ref, keys_ref, xt_ref, thr_ref, a1_ref, s2_ref, b2_ref, q_scr):
    x = x_ref[...]
    xt_ref[...] = x.T.astype(BF16)
    q = jnp.dot(x.astype(BF16), wq_ref[...], preferred_element_type=F32).astype(BF16)
    for hc in range(2 * PEER_HEADS):
        q_scr[hc] = q[:, hc * PEER_D_SUB:(hc + 1) * PEER_D_SUB]

    def head(h, carry):
        for tc in range(TOK_TILE // LANES):
            rows = slice(tc * LANES, (tc + 1) * LANES)
            nt = (((1,), (1,)), ((), ()))
            s1 = lax.dot_general(keys_ref[h, 0], q_scr[2 * h, rows, :], nt,
                                 preferred_element_type=F32)
            s2 = lax.dot_general(keys_ref[h, 1], q_scr[2 * h + 1, rows, :], nt,
                                 preferred_element_type=F32)
            a = _top16_of_keys(s1)
            b = _top16_of_keys(s2)
            top = _top16_pair_sums(a, b)
            m = a[0] + b[0]
            z = jnp.exp(top[0] - m)
            for k in range(1, 16):
                z = z + jnp.exp(top[k] - m)
            theta = top[15][0:1, :]
            inv_z = (1.0 / z)[0:1, :]
            thr_ref[h, :, rows] = theta - s1
            a1_ref[h, :, rows] = jnp.exp(s1 - a[0][0:1, :]) * inv_z
            s2_ref[h, :, rows] = s2
            b2_ref[h, :, rows] = jnp.exp(s2 - b[0][0:1, :])
        return carry

    lax.fori_loop(0, PEER_HEADS, head, 0)


def _gelu_tanh(x):
    return 0.5 * x * (1.0 + jnp.tanh(math.sqrt(2.0 / math.pi) * (x + 0.044715 * (x * x * x))))


def _peer_dense_kernel(xt_ref, thr_ref, a1_ref, s2_ref, b2_ref, u_ref, vt_ref, yt_ref, act_scr, w_scr):
    e = pl.program_id(1)
    rows_per_step = EXP_BLOCK // PEER_N_KEYS

    @pl.when(e == 0)
    def _():
        yt_ref[...] = jnp.zeros_like(yt_ref)

    act_scr[...] = jnp.dot(u_ref[...], xt_ref[...], preferred_element_type=F32)

    def token_chunk(tc, carry):
        lanes = pl.ds(pl.multiple_of(tc * LANES, LANES), LANES)
        for r in range(rows_per_step):
            rows = slice(r * PEER_N_KEYS, (r + 1) * PEER_N_KEYS)
            w = jnp.zeros((PEER_N_KEYS, LANES), F32)
            for h in range(PEER_HEADS):
                thr = thr_ref[h, r:r + 1, lanes]
                a1 = a1_ref[h, r:r + 1, lanes]
                w = w + jnp.where(s2_ref[h, :, lanes] >= thr, b2_ref[h, :, lanes], 0.0) * a1
            w_scr[rows, lanes] = (w * _gelu_tanh(act_scr[rows, lanes])).astype(BF16)
        return carry

    lax.fori_loop(0, TOK_TILE // LANES, token_chunk, 0)
    yt_ref[...] += jnp.dot(vt_ref[...], w_scr[...], preferred_element_type=F32)


def _resid_ln_t_kernel(x_ref, yt_ref, g_ref, b_ref, o_ref):
    z = DN_ALPHA * x_ref[...] + yt_ref[...].T
    mu = jnp.mean(z, axis=-1, keepdims=True)
    zc = z - mu
    var = jnp.mean(zc * zc, axis=-1, keepdims=True)
    o_ref[...] = zc * lax.rsqrt(var + LN_EPS) * g_ref[...] + b_ref[...]


def _peer_layer(x, wq, keys, u, vt, ln_g, ln_b):
    n_tok = x.shape[0]
    n_tiles = n_tok // TOK_TILE
    n_exp = u.shape[0]
    fac = jax.ShapeDtypeStruct((PEER_HEADS, PEER_N_KEYS, n_tok), F32)
    fac_spec = pl.BlockSpec((PEER_HEADS, PEER_N_KEYS, TOK_TILE), lambda t: (0, 0, t))
    xt, thr, a1, s2, b2 = pl.pallas_call(
        _peer_select_kernel,
        grid=(n_tiles,),
        in_specs=[
            pl.BlockSpec((TOK_TILE, D_MODEL), lambda t: (t, 0)),
            pl.BlockSpec((D_MODEL, PEER_Q_W), lambda t: (0, 0)),
            pl.BlockSpec((PEER_HEADS, 2, PEER_N_KEYS, PEER_D_SUB), lambda t: (0, 0, 0, 0)),
        ],
        out_specs=[pl.BlockSpec((D_MODEL, TOK_TILE), lambda t: (0, t))] + [fac_spec] * 4,
        out_shape=[jax.ShapeDtypeStruct((D_MODEL, n_tok), BF16)] + [fac] * 4,
        scratch_shapes=[pltpu.VMEM((2 * PEER_HEADS, TOK_TILE, PEER_D_SUB), BF16)],
        compiler_params=pltpu.CompilerParams(dimension_semantics=("parallel",),
                                             vmem_limit_bytes=VMEM_LIMIT),
        name="peer_select",
    )(x, wq, keys)

    fac_spec2 = pl.BlockSpec((PEER_HEADS, PEER_N_KEYS, TOK_TILE), lambda t, e: (0, 0, t))
    row_spec = pl.BlockSpec((PEER_HEADS, EXP_BLOCK // PEER_N_KEYS, TOK_TILE), lambda t, e: (0, e, t))
    yt = pl.pallas_call(
        _peer_dense_kernel,
        grid=(n_tiles, n_exp // EXP_BLOCK),
        in_specs=[pl.BlockSpec((D_MODEL, TOK_TILE), lambda t, e: (0, t)),
                  row_spec, row_spec, fac_spec2, fac_spec2] + [
            pl.BlockSpec((EXP_BLOCK, D_MODEL), lambda t, e: (e, 0)),
            pl.BlockSpec((D_MODEL, EXP_BLOCK), lambda t, e: (0, e)),
        ],
        out_specs=pl.BlockSpec((D_MODEL, TOK_TILE), lambda t, e: (0, t)),
        out_shape=jax.ShapeDtypeStruct((D_MODEL, n_tok), F32),
        scratch_shapes=[pltpu.VMEM((EXP_BLOCK, TOK_TILE), F32), pltpu.VMEM((EXP_BLOCK, TOK_TILE), BF16)],
        compiler_params=pltpu.CompilerParams(dimension_semantics=("parallel", "arbitrary"),
                                             vmem_limit_bytes=VMEM_LIMIT),
        name="peer_dense",
    )(xt, thr, a1, s2, b2, u, vt)

    return pl.pallas_call(
        _resid_ln_t_kernel,
        grid=(n_tiles,),
        in_specs=[
            pl.BlockSpec((TOK_TILE, D_MODEL), lambda t: (t, 0)),
            pl.BlockSpec((D_MODEL, TOK_TILE), lambda t: (0, t)),
            pl.BlockSpec((1, D_MODEL), lambda t: (0, 0)),
            pl.BlockSpec((1, D_MODEL), lambda t: (0, 0)),
        ],
        out_specs=pl.BlockSpec((TOK_TILE, D_MODEL), lambda t: (t, 0)),
        out_shape=jax.ShapeDtypeStruct((n_tok, D_MODEL), F32),
        compiler_params=pltpu.CompilerParams(dimension_semantics=("parallel",)),
        name="peer_resid_ln",
    )(x, yt, ln_g.reshape(1, D_MODEL), ln_b.reshape(1, D_MODEL))


GDN_HEADS = 4
GDN_DK = 128
GDN_DV = 128
GDN_CHUNK = 64
CONV_W = 4
GDN_QK_W = GDN_HEADS * GDN_DK
GDN_V_W = GDN_HEADS * GDN_DV
GDN_CONV_CH = 2 * GDN_QK_W + GDN_V_W
LRU_WIDTH = 512
LRU_BLOCKS = 8
LRU_C = 8.0
SWA_HEADS = 16
SWA_KV_HEADS = 4
SWA_HEAD_DIM = 64
SWA_GROUP = SWA_HEADS // SWA_KV_HEADS
SWA_Q_W = SWA_HEADS * SWA_HEAD_DIM
SWA_KV_W = SWA_KV_HEADS * SWA_HEAD_DIM
WINDOW = 128
REL_BUCKETS = 32
REL_MAX_DIST = 128
MASKED = -1e30

SEQ_TILE = 256
ROW_TILE = 256
DEC_TILE = 8

_NT = (((1,), (1,)), ((), ()))
_TN = (((0,), (0,)), ((), ()))


def _mm(a, b):
    return jnp.dot(a.astype(BF16), b.astype(BF16), preferred_element_type=F32)


def _mm_nt(a, b):
    return lax.dot_general(a.astype(BF16), b.astype(BF16), _NT, preferred_element_type=F32)


def _mm_tn(a, b):
    return lax.dot_general(a.astype(BF16), b.astype(BF16), _TN, preferred_element_type=F32)


def _mm_f32(a, b):
    return jnp.dot(a, b, preferred_element_type=F32, precision=lax.Precision.HIGHEST)


def _sigmoid(x):
    return 1.0 / (1.0 + jnp.exp(-x))


def _silu(x):
    return x * _sigmoid(x)


def _softplus(x):
    return jnp.maximum(x, 0.0) + jnp.log1p(jnp.exp(-jnp.abs(x)))


def _layer_norm(z, g, b):
    mu = jnp.mean(z, axis=-1, keepdims=True)
    zc = z - mu
    var = jnp.mean(zc * zc, axis=-1, keepdims=True)
    return zc * lax.rsqrt(var + LN_EPS) * g + b


def _l2_normalize(x):
    return x * lax.rsqrt(jnp.sum(x * x, axis=-1, keepdims=True) + 1e-6)


def _causal_conv(x, xp_scr, w_ref, n_rows):
    xp_scr[8:8 + n_rows, :] = x
    y = x * w_ref[CONV_W - 1:CONV_W, :]
    for i in range(CONV_W - 1):
        y = y + xp_scr[5 + i:5 + i + n_rows, :] * w_ref[i:i + 1, :]
    xp_scr[0:8, :] = x[n_rows - 8:n_rows, :]
    return y


def _gdn_gates(ab, alog_row, dtb_row):
    g = -jnp.exp(alog_row) * _softplus(ab + dtb_row)
    return g, _sigmoid(ab)


def _gdn_out_norm(o, z, nw_row):
    o = o * lax.rsqrt(jnp.mean(o * o, axis=-1, keepdims=True) + 1e-6) * nw_row
    return o * _silu(z)


def _lru_coeffs(xc, wr_ref, br_ref, wi_ref, bi_ref, lam_ref):
    r = _sigmoid(_mm(xc, wr_ref[...]) + br_ref[...])
    i = _sigmoid(_mm(xc, wi_ref[...]) + bi_ref[...])
    log_a = -LRU_C * r * _softplus(-lam_ref[...])
    a = jnp.exp(log_a)
    one_minus_a2 = -jnp.tanh(log_a) * (a * a + 1.0)
    return a, jnp.sqrt(one_minus_a2) * (i * xc)


def _proj_kernel(x_ref, w_ref, b_ref, o_ref):
    o_ref[...] = _mm(x_ref[...], w_ref[...]) + b_ref[...]


def _project(x, w, b):
    n, k = x.shape
    m = w.shape[1]
    return pl.pallas_call(
        _proj_kernel,
        grid=(n // ROW_TILE,),
        in_specs=[pl.BlockSpec((ROW_TILE, k), lambda t: (t, 0)),
                  pl.BlockSpec((k, m), lambda t: (0, 0)),
                  pl.BlockSpec((1, m), lambda t: (0, 0))],
        out_specs=pl.BlockSpec((ROW_TILE, m), lambda t: (t, 0)),
        out_shape=jax.ShapeDtypeStruct((n, m), F32),
        compiler_params=pltpu.CompilerParams(dimension_semantics=("parallel",),
                                             vmem_limit_bytes=VMEM_LIMIT),
        name="proj",
    )(x, w, b.reshape(1, m))


def _outproj_ln_kernel(x_ref, oa_ref, ob_ref, w_ref, b_ref, g_ref, beta_ref, o_ref):
    half = oa_ref.shape[1]
    y = _mm(oa_ref[...], w_ref[0:half, :]) + _mm(ob_ref[...], w_ref[half:2 * half, :]) + b_ref[...]
    o_ref[...] = _layer_norm(DN_ALPHA * x_ref[...] + y, g_ref[...], beta_ref[...])


def _outproj_ln(x, oa, oa_col, ob, ob_col, w, b, g, beta, tile):
    n = oa.shape[0]
    half = w.shape[0] // 2
    row = lambda t: (0, 0)
    return pl.pallas_call(
        _outproj_ln_kernel,
        grid=(n // tile,),
        in_specs=[pl.BlockSpec((tile, D_MODEL), lambda t: (t, 0)),
                  pl.BlockSpec((tile, half), lambda t: (t, oa_col)),
                  pl.BlockSpec((tile, half), lambda t: (t, ob_col)),
                  pl.BlockSpec((2 * half, D_MODEL), row),
                  pl.BlockSpec((1, D_MODEL), row), pl.BlockSpec((1, D_MODEL), row),
                  pl.BlockSpec((1, D_MODEL), row)],
        out_specs=pl.BlockSpec((tile, D_MODEL), lambda t: (t, 0)),
        out_shape=jax.ShapeDtypeStruct((n, D_MODEL), F32),
        compiler_params=pltpu.CompilerParams(dimension_semantics=("parallel",),
                                             vmem_limit_bytes=VMEM_LIMIT),
        name="outproj_ln",
    )(x, oa, ob, w, b.reshape(1, D_MODEL), g.reshape(1, D_MODEL), beta.reshape(1, D_MODEL))


def _gdn_prompt_kernel(qkv_ref, ab_ref, z_ref, cw_ref, alog_ref, dtb_ref, nw_ref, o_ref, s_out_ref,
                       xp_scr, s_scr, q_scr, k_scr, v_scr, gc_scr, beta_scr):
    t = pl.program_id(1)

    @pl.when(t == 0)
    def _():
        xp_scr[0:8, :] = jnp.zeros((8, GDN_CONV_CH), F32)
        s_scr[...] = jnp.zeros_like(s_scr)

    y = _silu(_causal_conv(qkv_ref[...], xp_scr, cw_ref, SEQ_TILE))
    for h in range(GDN_HEADS):
        cols = slice(h * GDN_DK, (h + 1) * GDN_DK)
        q_scr[:, cols] = _l2_normalize(y[:, cols]) * (GDN_DK ** -0.5)
        k_scr[:, cols] = _l2_normalize(y[:, GDN_QK_W + h * GDN_DK:GDN_QK_W + (h + 1) * GDN_DK])
    v_scr[...] = y[:, 2 * GDN_QK_W:]
    g, beta = _gdn_gates(ab_ref[...], alog_ref[...], dtb_ref[...])
    beta_scr[...] = beta
    pos = lax.broadcasted_iota(jnp.int32, g.shape, 0) % GDN_CHUNK
    shift = 1
    while shift < GDN_CHUNK:
        g = g + jnp.where(pos >= shift, pltpu.roll(g, shift, 0), 0.0)
        shift *= 2
    gc_scr[...] = g

    ri = lax.broadcasted_iota(jnp.int32, (GDN_CHUNK, GDN_CHUNK), 0)
    ci = lax.broadcasted_iota(jnp.int32, (GDN_CHUNK, GDN_CHUNK), 1)
    eye = ri == ci
    incl = ri >= ci
    strict = ri > ci

    def chunk(c, carry):
        rows = pl.ds(pl.multiple_of(c * GDN_CHUNK, GDN_CHUNK), GDN_CHUNK)
        gc_all = gc_scr[rows, :]
        beta_all = beta_scr[rows, :]
        for h in range(GDN_HEADS):
            cols = slice(h * GDN_DK, (h + 1) * GDN_DK)
            q = q_scr[rows, cols]
            k = k_scr[rows, cols]
            v = v_scr[rows, cols]
            gcol = gc_all[:, h:h + 1]
            bcol = beta_all[:, GDN_HEADS + h:GDN_HEADS + h + 1]
            grow = jnp.sum(jnp.where(eye, gcol, 0.0), axis=0, keepdims=True)
            decay = jnp.where(incl, jnp.exp(jnp.where(incl, gcol - grow, 0.0)), 0.0)
            kb = k * bcol
            neg = -jnp.where(strict, _mm_nt(kb, k) * decay, 0.0)
            inv = jnp.where(eye, 1.0, 0.0) + neg
            pw = neg
            for _ in range(5):
                pw = _mm_f32(pw, pw)
                inv = inv + _mm_f32(inv, pw)
            egc = jnp.exp(gcol)
            rhs = jnp.concatenate([v * bcol, kb * egc], axis=1)
            sol = _mm_f32(inv, rhs)
            u, w = sol[:, :GDN_DV], sol[:, GDN_DV:]
            qk = jnp.where(incl, _mm_nt(q, k) * decay, 0.0)
            s = s_scr[h]
            v_new = u - _mm(w, s)
            o = _mm(q * egc, s) + _mm(qk, v_new)
            g_last = gcol[GDN_CHUNK - 1:GDN_CHUNK, :]
            s_scr[h] = s * jnp.exp(g_last) + _mm_tn(k * jnp.exp(g_last - gcol), v_new)
            o_ref[rows, cols] = _gdn_out_norm(o, z_ref[rows, cols], nw_ref[...])
        return carry

    lax.fori_loop(0, SEQ_TILE // GDN_CHUNK, chunk, 0)

    @pl.when(t == pl.num_programs(1) - 1)
    def _():
        s_out_ref[0] = s_scr[...]


def _gdn_prompt(proj, bsz, t_len, cw, alog_row, dtb_row, nw_row):
    n_t = t_len // SEQ_TILE
    row = lambda b, t: (0, 0)
    return pl.pallas_call(
        _gdn_prompt_kernel,
        grid=(bsz, n_t),
        in_specs=[pl.BlockSpec((SEQ_TILE, GDN_CONV_CH), lambda b, t: (b * n_t + t, 0)),
                  pl.BlockSpec((SEQ_TILE, LANES), lambda b, t: (b * n_t + t, 24)),
                  pl.BlockSpec((SEQ_TILE, GDN_V_W), lambda b, t: (b * n_t + t, 3)),
                  pl.BlockSpec((CONV_W, GDN_CONV_CH), row),
                  pl.BlockSpec((1, LANES), row), pl.BlockSpec((1, LANES), row),
                  pl.BlockSpec((1, GDN_DV), row)],
        out_specs=[pl.BlockSpec((SEQ_TILE, GDN_V_W), lambda b, t: (b * n_t + t, 0)),
                   pl.BlockSpec((1, GDN_HEADS, GDN_DK, GDN_DV), lambda b, t: (b, 0, 0, 0))],
        out_shape=[jax.ShapeDtypeStruct((bsz * t_len, GDN_V_W), F32),
                   jax.ShapeDtypeStruct((bsz, GDN_HEADS, GDN_DK, GDN_DV), F32)],
        scratch_shapes=[pltpu.VMEM((SEQ_TILE + 8, GDN_CONV_CH), F32),
                        pltpu.VMEM((GDN_HEADS, GDN_DK, GDN_DV), F32),
                        pltpu.VMEM((SEQ_TILE, GDN_QK_W), F32), pltpu.VMEM((SEQ_TILE, GDN_QK_W), F32),
                        pltpu.VMEM((SEQ_TILE, GDN_V_W), F32),
                        pltpu.VMEM((SEQ_TILE, LANES), F32), pltpu.VMEM((SEQ_TILE, LANES), F32)],
        compiler_params=pltpu.CompilerParams(dimension_semantics=("parallel", "arbitrary"),
                                             vmem_limit_bytes=VMEM_LIMIT),
        name="gdn_prompt",
    )(proj, proj, proj, cw, alog_row, dtb_row, nw_row)


def _lru_prompt_kernel(xr_ref, gate_ref, cw_ref, cb_ref, wr_ref, br_ref, wi_ref, bi_ref, lam_ref,
                       o_ref, h_out_ref, xp_scr, h_scr):
    t = pl.program_id(1)

    @pl.when(t == 0)
    def _():
        xp_scr[0:8, :] = jnp.zeros((8, LRU_WIDTH), F32)
        h_scr[...] = jnp.zeros_like(h_scr)

    xc = _causal_conv(xr_ref[...], xp_scr, cw_ref, SEQ_TILE) + cb_ref[...]
    a, b = _lru_coeffs(xc, wr_ref, br_ref, wi_ref, bi_ref, lam_ref)
    pos = lax.broadcasted_iota(jnp.int32, a.shape, 0)
    shift = 1
    while shift < SEQ_TILE:
        valid = pos >= shift
        b = jnp.where(valid, a * pltpu.roll(b, shift, 0) + b, b)
        a = jnp.where(valid, a * pltpu.roll(a, shift, 0), a)
        shift *= 2
    h = b + a * h_scr[...]
    h_scr[...] = h[SEQ_TILE - 1:SEQ_TILE, :]
    o_ref[...] = jax.nn.gelu(gate_ref[...]) * h

    @pl.when(t == pl.num_programs(1) - 1)
    def _():
        h_out_ref[0] = h[SEQ_TILE - 1:SEQ_TILE, :]


def _lru_prompt(proj, bsz, t_len, lru_w):
    n_t = t_len // SEQ_TILE
    row = lambda b, t: (0, 0)
    wide = pl.BlockSpec((1, LRU_WIDTH), row)
    sq = pl.BlockSpec((LRU_WIDTH, LRU_WIDTH), row)
    return pl.pallas_call(
        _lru_prompt_kernel,
        grid=(bsz, n_t),
        in_specs=[pl.BlockSpec((SEQ_TILE, LRU_WIDTH), lambda b, t: (b * n_t + t, 4)),
                  pl.BlockSpec((SEQ_TILE, LRU_WIDTH), lambda b, t: (b * n_t + t, 5)),
                  pl.BlockSpec((CONV_W, LRU_WIDTH), row), wide, sq, wide, sq, wide, wide],
        out_specs=[pl.BlockSpec((SEQ_TILE, LRU_WIDTH), lambda b, t: (b * n_t + t, 0)),
                   pl.BlockSpec((1, 1, LRU_WIDTH), lambda b, t: (b, 0, 0))],
        out_shape=[jax.ShapeDtypeStruct((bsz * t_len, LRU_WIDTH), F32),
                   jax.ShapeDtypeStruct((bsz, 1, LRU_WIDTH), F32)],
        scratch_shapes=[pltpu.VMEM((SEQ_TILE + 8, LRU_WIDTH), F32), pltpu.VMEM((1, LRU_WIDTH), F32)],
        compiler_params=pltpu.CompilerParams(dimension_semantics=("parallel", "arbitrary"),
                                             vmem_limit_bytes=VMEM_LIMIT),
        name="lru_prompt",
    )(proj, proj, *lru_w)


def _ab_sample_kernel(qkv_ref, ab_ref, z_ref, xr_ref, gate_ref, gbuf_ref, lbuf_ref, s_ref, h0_ref,
                      cw_ref, alog_ref, dtb_ref, nw_ref,
                      lcw_ref, lcb_ref, wr_ref, br_ref, wi_ref, bi_ref, lam_ref,
                      oa_ref, ob_ref, s_out_ref, h_out_ref, o_scr):
    y = qkv_ref[...] * cw_ref[CONV_W - 1:CONV_W, :]
    for i in range(CONV_W - 1):
        y = y + gbuf_ref[i] * cw_ref[i:i + 1, :]
    y = _silu(y)
    g, beta = _gdn_gates(ab_ref[...], alog_ref[...], dtb_ref[...])
    eg = jnp.exp(g)
    for h in range(GDN_HEADS):
        cols = slice(h * GDN_DK, (h + 1) * GDN_DK)
        q = _l2_normalize(y[:, cols]) * (GDN_DK ** -0.5)
        k = _l2_normalize(y[:, GDN_QK_W + h * GDN_DK:GDN_QK_W + (h + 1) * GDN_DK])
        v = y[:, 2 * GDN_QK_W + h * GDN_DV:2 * GDN_QK_W + (h + 1) * GDN_DV]
        qk = jnp.sum(q * k, axis=-1, keepdims=True)
        q_t = q.T
        k_t = k.T
        for b in range(DEC_TILE):
            s = s_ref[b, h]
            kcol = k_t[:, b:b + 1]
            e = eg[b:b + 1, h:h + 1]
            ks = jnp.sum(s * kcol, axis=0, keepdims=True)
            qs = jnp.sum(s * q_t[:, b:b + 1], axis=0, keepdims=True)
            v_new = beta[b:b + 1, GDN_HEADS + h:GDN_HEADS + h + 1] * (v[b:b + 1, :] - e * ks)
            o_scr[b:b + 1, cols] = e * qs + qk[b:b + 1, :] * v_new
            s_out_ref[b, h] = e * s + kcol * v_new
    for h in range(GDN_HEADS):
        cols = slice(h * GDN_DV, (h + 1) * GDN_DV)
        oa_ref[:, cols] = _gdn_out_norm(o_scr[:, cols], z_ref[:, cols], nw_ref[...])
    xr = xr_ref[...]
    xc = xr * lcw_ref[CONV_W - 1:CONV_W, :] + lcb_ref[...]
    for i in range(CONV_W - 1):
        xc = xc + lbuf_ref[i] * lcw_ref[i:i + 1, :]
    a, bb = _lru_coeffs(xc, wr_ref, br_ref, wi_ref, bi_ref, lam_ref)
    hid = a * h0_ref[...] + bb
    h_out_ref[...] = hid
    ob_ref[...] = jax.nn.gelu(gate_ref[...]) * hid


def _ab_sample(proj, row0, n_dec, gbuf_t, lbuf_t, s0, h0, gdn_w, lru_w):
    blk0 = row0 // DEC_TILE
    row = lambda i: (0, 0)
    wide = pl.BlockSpec((1, LRU_WIDTH), row)
    sq = pl.BlockSpec((LRU_WIDTH, LRU_WIDTH), row)
    return pl.pallas_call(
        _ab_sample_kernel,
        grid=(n_dec // DEC_TILE,),
        in_specs=[pl.BlockSpec((DEC_TILE, GDN_CONV_CH), lambda i: (blk0 + i, 0)),
                  pl.BlockSpec((DEC_TILE, LANES), lambda i: (blk0 + i, 24)),
                  pl.BlockSpec((DEC_TILE, GDN_V_W), lambda i: (blk0 + i, 3)),
                  pl.BlockSpec((DEC_TILE, LRU_WIDTH), lambda i: (blk0 + i, 4)),
                  pl.BlockSpec((DEC_TILE, LRU_WIDTH), lambda i: (blk0 + i, 5)),
                  pl.BlockSpec((CONV_W - 1, DEC_TILE, GDN_CONV_CH), lambda i: (0, i, 0)),
                  pl.BlockSpec((CONV_W - 1, DEC_TILE, LRU_WIDTH), lambda i: (0, i, 0)),
                  pl.BlockSpec((DEC_TILE, GDN_HEADS, GDN_DK, GDN_DV), lambda i: (i, 0, 0, 0)),
                  pl.BlockSpec((DEC_TILE, LRU_WIDTH), lambda i: (i, 0)),
                  pl.BlockSpec((CONV_W, GDN_CONV_CH), row),
                  pl.BlockSpec((1, LANES), row), pl.BlockSpec((1, LANES), row),
                  pl.BlockSpec((1, GDN_DV), row),
                  pl.BlockSpec((CONV_W, LRU_WIDTH), row), wide, sq, wide, sq, wide, wide],
        out_specs=[pl.BlockSpec((DEC_TILE, GDN_V_W), lambda i: (i, 0)),
                   pl.BlockSpec((DEC_TILE, LRU_WIDTH), lambda i: (i, 0)),
                   pl.BlockSpec((DEC_TILE, GDN_HEADS, GDN_DK, GDN_DV), lambda i: (i, 0, 0, 0)),
                   pl.BlockSpec((DEC_TILE, LRU_WIDTH), lambda i: (i, 0))],
        out_shape=[jax.ShapeDtypeStruct((n_dec, GDN_V_W), F32),
                   jax.ShapeDtypeStruct((n_dec, LRU_WIDTH), F32),
                   jax.ShapeDtypeStruct((n_dec, GDN_HEADS, GDN_DK, GDN_DV), F32),
                   jax.ShapeDtypeStruct((n_dec, LRU_WIDTH), F32)],
        scratch_shapes=[pltpu.VMEM((DEC_TILE, GDN_V_W), F32)],
        compiler_params=pltpu.CompilerParams(dimension_semantics=("parallel",),
                                             vmem_limit_bytes=VMEM_LIMIT),
        name="ab_sample",
    )(proj, proj, proj, proj, proj, gbuf_t, lbuf_t, s0, h0, *gdn_w, *lru_w)


def _softmax_sink_pv(logits, sink, v, extra_logit=None, extra_v=None):
    m = jnp.maximum(jnp.max(logits, axis=-1, keepdims=True), sink)
    if extra_logit is not None:
        m = jnp.maximum(m, extra_logit)
    p = jnp.exp(logits - m)
    den = jnp.sum(p, axis=-1, keepdims=True) + jnp.exp(sink - m)
    acc = _mm(p, v)
    if extra_logit is not None:
        pe = jnp.exp(extra_logit - m)
        den = den + pe
        acc = acc + pe.astype(BF16).astype(F32) * extra_v.astype(BF16).astype(F32)
    return acc / den


def _swa_prompt_kernel(q_ref, kc_ref, kp_ref, vc_ref, vp_ref, bias_ref, sink_ref, o_ref):
    n = pl.program_id(1)
    kcat = jnp.concatenate([kp_ref[...], kc_ref[...]], axis=0)
    vcat = jnp.concatenate([vp_ref[...], vc_ref[...]], axis=0)
    qi = lax.broadcasted_iota(jnp.int32, (WINDOW, 2 * WINDOW), 0)
    ki = lax.broadcasted_iota(jnp.int32, (WINDOW, 2 * WINDOW), 1)
    rel = qi + WINDOW - ki
    mask = (rel >= 0) & (rel < WINDOW) & ((ki >= WINDOW) | (n > 0))
    for h in range(SWA_HEADS):
        g = h // SWA_GROUP
        kv_cols = slice(g * SWA_HEAD_DIM, (g + 1) * SWA_HEAD_DIM)
        q = q_ref[:, h * SWA_HEAD_DIM:(h + 1) * SWA_HEAD_DIM]
        logits = _mm_nt(q, kcat[:, kv_cols]) * (SWA_HEAD_DIM ** -0.5) + bias_ref[h]
        logits = jnp.where(mask, logits, MASKED)
        o_ref[:, h * SWA_HEAD_DIM:(h + 1) * SWA_HEAD_DIM] = _softmax_sink_pv(
            logits, sink_ref[h:h + 1, 0:1], vcat[:, kv_cols])


def _swa_prompt(proj, bsz, t_len, bias_tab, sink_tab):
    n_blk = t_len // WINDOW
    cur = lambda col: (lambda b, n: (b * n_blk + n, col))
    prev = lambda col: (lambda b, n: (b * n_blk + jnp.maximum(n - 1, 0), col))
    return pl.pallas_call(
        _swa_prompt_kernel,
        grid=(bsz, n_blk),
        in_specs=[pl.BlockSpec((WINDOW, SWA_Q_W), cur(0)),
                  pl.BlockSpec((WINDOW, SWA_KV_W), cur(4)), pl.BlockSpec((WINDOW, SWA_KV_W), prev(4)),
                  pl.BlockSpec((WINDOW, SWA_KV_W), cur(5)), pl.BlockSpec((WINDOW, SWA_KV_W), prev(5)),
                  pl.BlockSpec((SWA_HEADS, WINDOW, 2 * WINDOW), lambda b, n: (0, 0, 0)),
                  pl.BlockSpec((SWA_HEADS, LANES), lambda b, n: (0, 0))],
        out_specs=pl.BlockSpec((WINDOW, SWA_Q_W), lambda b, n: (b * n_blk + n, 0)),
        out_shape=jax.ShapeDtypeStruct((bsz * t_len, SWA_Q_W), F32),
        compiler_params=pltpu.CompilerParams(dimension_semantics=("parallel", "arbitrary"),
                                             vmem_limit_bytes=VMEM_LIMIT),
        name="swa_prompt",
    )(proj, proj, proj, proj, proj, bias_tab, sink_tab)


def _swa_sample_kernel(q_ref, kn_ref, vn_ref, kc_ref, vc_ref, bias_ref, bias0_ref, sink_ref, o_ref):
    lane = lax.broadcasted_iota(jnp.int32, (SWA_GROUP, WINDOW), 1)
    for b in range(DEC_TILE):
        qb = q_ref[b]
        for g in range(SWA_KV_HEADS):
            hs = slice(g * SWA_GROUP, (g + 1) * SWA_GROUP)
            kv_cols = slice(g * SWA_HEAD_DIM, (g + 1) * SWA_HEAD_DIM)
            q = qb[hs, :]
            logits = _mm_nt(q, kc_ref[b, :, kv_cols]) * (SWA_HEAD_DIM ** -0.5) + bias_ref[hs, :]
            logits = jnp.where(lane >= 1, logits, MASKED)
            kn = kn_ref[b:b + 1, kv_cols]
            own = jnp.sum(q.astype(BF16).astype(F32) * kn.astype(BF16).astype(F32), axis=-1,
                          keepdims=True) * (SWA_HEAD_DIM ** -0.5) + bias0_ref[hs, 0:1]
            o_ref[b, hs, :] = _softmax_sink_pv(logits, sink_ref[hs, 0:1], vc_ref[b, :, kv_cols],
                                               own, vn_ref[b:b + 1, kv_cols])


def _swa_sample(q3, kn, vn, kc, vc, bias_dec, bias0, sink_tab):
    n_dec = q3.shape[0]
    tab = lambda i: (0, 0)
    return pl.pallas_call(
        _swa_sample_kernel,
        grid=(n_dec // DEC_TILE,),
        in_specs=[pl.BlockSpec((DEC_TILE, SWA_HEADS, SWA_HEAD_DIM), lambda i: (i, 0, 0)),
                  pl.BlockSpec((DEC_TILE, SWA_KV_W), lambda i: (i, 0)),
                  pl.BlockSpec((DEC_TILE, SWA_KV_W), lambda i: (i, 0)),
                  pl.BlockSpec((DEC_TILE, WINDOW, SWA_KV_W), lambda i: (i, 0, 0)),
                  pl.BlockSpec((DEC_TILE, WINDOW, SWA_KV_W), lambda i: (i, 0, 0)),
                  pl.BlockSpec((SWA_HEADS, WINDOW), tab), pl.BlockSpec((SWA_HEADS, LANES), tab),
                  pl.BlockSpec((SWA_HEADS, LANES), tab)],
        out_specs=pl.BlockSpec((DEC_TILE, SWA_HEADS, SWA_HEAD_DIM), lambda i: (i, 0, 0)),
        out_shape=jax.ShapeDtypeStruct((n_dec, SWA_HEADS, SWA_HEAD_DIM), F32),
        compiler_params=pltpu.CompilerParams(dimension_semantics=("parallel",),
                                             vmem_limit_bytes=VMEM_LIMIT),
        name="swa_sample",
    )(q3, kn, vn, kc, vc, bias_dec, bias0, sink_tab)


def _t5_bucket(rel):
    exact = REL_BUCKETS // 2
    nf = jnp.maximum(rel, 1).astype(F32)
    large = exact + (jnp.log(nf / exact) / math.log(REL_MAX_DIST / exact)
                     * (REL_BUCKETS - exact)).astype(jnp.int32)
    return jnp.where(rel < exact, rel, jnp.minimum(large, REL_BUCKETS - 1))


def _lane_row(v, width=LANES):
    return jnp.zeros((1, width), F32).at[0, :v.shape[0]].set(v.astype(F32))


def kernel(x_prompt, x_sample, state_gdn, state_gdn_conv, state_lru, state_lru_conv, cache_swa_k, cache_swa_v, w_in_ab, gdn_conv_w, gdn_a_log, gdn_dt_bias, gdn_norm_w, lru_conv_w, lru_conv_b, lru_w_r, lru_b_r, lru_w_i, lru_b_i, lru_lam, w_out_ab, w_in_c, b_in_c, swa_sinks, w_out_c, b_out_c, rel_bias, ln_mix_g, ln_mix_b, ln_ffn_g, ln_ffn_b, peer_w_q, peer_keys, peer_u, peer_v):
    bsz, t_len, _ = x_prompt.shape
    n_dec = x_sample.shape[0]
    n_prompt = bsz * t_len
    assert x_sample.shape[1] == 1 and cache_swa_k.shape[2] == WINDOW

    def peer_ffn(x, layer):
        return _peer_layer(x, peer_w_q[layer].astype(BF16), peer_keys[layer].astype(BF16),
                           peer_u[layer].astype(BF16), peer_v[layer].T.astype(BF16),
                           ln_ffn_g[layer], ln_ffn_b[layer])

    x = _tokens(x_prompt.reshape(n_prompt, D_MODEL), x_sample.reshape(n_dec, D_MODEL))
    (x1_p, x1_s, p_gdn, p_gdn_conv, p_lru, p_lru_conv, s_gdn, s_gdn_conv, s_lru,
     s_lru_conv) = _layer0_mixers(
        x, bsz, t_len, n_dec, state_gdn, state_gdn_conv, state_lru, state_lru_conv, w_in_ab,
        gdn_conv_w, gdn_a_log, gdn_dt_bias, gdn_norm_w, lru_conv_w, lru_conv_b, lru_w_r, lru_b_r,
        lru_w_i, lru_b_i, lru_lam, w_out_ab, ln_mix_g, ln_mix_b)
    x = peer_ffn(_tokens(x1_p, x1_s), 0)
    x1_p, x1_s, p_k, p_v, s_k, s_v = _layer1_mixers(
        x, bsz, t_len, n_dec, cache_swa_k, cache_swa_v, w_in_c, b_in_c, swa_sinks, w_out_c, b_out_c,
        rel_bias, ln_mix_g, ln_mix_b)
    x = peer_ffn(_tokens(x1_p, x1_s), 1)

    n_real = n_prompt + n_dec
    lead = lambda a: a[None]
    return (x[:n_prompt].reshape(bsz, t_len, D_MODEL), x[n_prompt:n_real].reshape(n_dec, 1, D_MODEL),
            lead(p_gdn), lead(p_gdn_conv), lead(p_lru.reshape(bsz, LRU_WIDTH)), lead(p_lru_conv),
            lead(p_k), lead(p_v),
            lead(s_gdn), lead(s_gdn_conv), lead(s_lru), lead(s_lru_conv), lead(s_k), lead(s_v))


def _tokens(xp_rows, xs_rows):
    n_real = xp_rows.shape[0] + xs_rows.shape[0]
    n_tok = -(-n_real // TOK_TILE) * TOK_TILE
    return jnp.concatenate([xp_rows, xs_rows, jnp.zeros((n_tok - n_real, D_MODEL), F32)])


def _layer0_mixers(x, bsz, t_len, n_dec, state_gdn, state_gdn_conv, state_lru, state_lru_conv,
                   w_in_ab, gdn_conv_w, gdn_a_log, gdn_dt_bias, gdn_norm_w, lru_conv_w, lru_conv_b,
                   lru_w_r, lru_b_r, lru_w_i, lru_b_i, lru_lam, w_out_ab, ln_mix_g, ln_mix_b):
    n_prompt = bsz * t_len
    n_real = n_prompt + n_dec
    assert t_len % SEQ_TILE == 0 and n_dec % DEC_TILE == 0 and n_prompt % ROW_TILE == 0
    w_in = w_in_ab[0]
    c0 = GDN_CONV_CH + GDN_V_W
    c1 = c0 + 2 * GDN_HEADS
    w_all = jnp.concatenate([w_in[:, :c0], w_in[:, c1:], w_in[:, c0:c1],
                             jnp.zeros((D_MODEL, LANES - 2 * GDN_HEADS), F32)], axis=1).astype(BF16)
    proj = _project(x, w_all, jnp.zeros((w_all.shape[1],), F32))
    gdn_w = (gdn_conv_w[0], _lane_row(gdn_a_log[0]), _lane_row(gdn_dt_bias[0]),
             gdn_norm_w[0].reshape(1, GDN_DV))
    eye_b = jnp.eye(LRU_BLOCKS, dtype=F32)

    def block_diag(w):
        return (eye_b[:, None, :, None] * w[:, :, None, :]).reshape(LRU_WIDTH, LRU_WIDTH).astype(BF16)

    wide = lambda v: v.reshape(1, LRU_WIDTH)
    lru_w = (lru_conv_w[0], wide(lru_conv_b[0]), block_diag(lru_w_r[0]), wide(lru_b_r[0]),
             block_diag(lru_w_i[0]), wide(lru_b_i[0]), wide(lru_lam[0]))
    oa_p, p_gdn = _gdn_prompt(proj, bsz, t_len, *gdn_w)
    ob_p, p_lru = _lru_prompt(proj, bsz, t_len, lru_w)
    oa_s, ob_s, s_gdn, s_lru = _ab_sample(
        proj, n_prompt, n_dec, jnp.swapaxes(state_gdn_conv[0], 0, 1),
        jnp.swapaxes(state_lru_conv[0], 0, 1), state_gdn[0], state_lru[0], gdn_w, lru_w)
    w_out = w_out_ab[0].astype(BF16)
    zero_b = jnp.zeros((D_MODEL,), F32)
    x1_p = _outproj_ln(x, oa_p, 0, ob_p, 0, w_out, zero_b, ln_mix_g[0], ln_mix_b[0], ROW_TILE)
    x1_s = _outproj_ln(x[n_prompt:n_real], oa_s, 0, ob_s, 0, w_out, zero_b, ln_mix_g[0], ln_mix_b[0],
                       n_dec)
    pre = proj[:n_prompt].reshape(bsz, t_len, -1)[:, t_len - (CONV_W - 1):]
    p_gdn_conv = pre[:, :, :GDN_CONV_CH]
    p_lru_conv = pre[:, :, c0:c0 + LRU_WIDTH]
    new = proj[n_prompt:n_real]
    s_gdn_conv = jnp.concatenate([state_gdn_conv[0][:, 1:], new[:, None, :GDN_CONV_CH]], axis=1)
    s_lru_conv = jnp.concatenate([state_lru_conv[0][:, 1:], new[:, None, c0:c0 + LRU_WIDTH]], axis=1)
    return x1_p, x1_s, p_gdn, p_gdn_conv, p_lru, p_lru_conv, s_gdn, s_gdn_conv, s_lru, s_lru_conv


def _layer1_mixers(x, bsz, t_len, n_dec, cache_swa_k, cache_swa_v, w_in_c, b_in_c, swa_sinks,
                   w_out_c, b_out_c, rel_bias, ln_mix_g, ln_mix_b):
    n_prompt = bsz * t_len
    n_real = n_prompt + n_dec
    assert t_len % WINDOW == 0 and n_dec % DEC_TILE == 0 and n_prompt % ROW_TILE == 0
    proj = _project(x, w_in_c[0].astype(BF16), b_in_c[0])
    rel = jnp.arange(WINDOW)[:, None] + WINDOW - jnp.arange(2 * WINDOW)[None, :]
    bias_vec = rel_bias.astype(F32)[_t5_bucket(jnp.arange(WINDOW))]
    bias_tab = jnp.transpose(bias_vec[jnp.clip(rel, 0, WINDOW - 1)], (2, 0, 1))
    bias_dec = bias_vec[jnp.clip(WINDOW - jnp.arange(WINDOW), 0, WINDOW - 1)].T
    bias_own = jnp.broadcast_to(bias_vec[0][:, None], (SWA_HEADS, LANES))
    sink_tab = jnp.broadcast_to(swa_sinks[0].astype(F32)[:, None], (SWA_HEADS, LANES))
    attn_p = _swa_prompt(proj, bsz, t_len, bias_tab, sink_tab)
    new = proj[n_prompt:n_real]
    kn, vn = new[:, SWA_Q_W:SWA_Q_W + SWA_KV_W], new[:, SWA_Q_W + SWA_KV_W:]
    kc = cache_swa_k[0].reshape(n_dec, WINDOW, SWA_KV_W)
    vc = cache_swa_v[0].reshape(n_dec, WINDOW, SWA_KV_W)
    attn_s = _swa_sample(new[:, :SWA_Q_W].reshape(n_dec, SWA_HEADS, SWA_HEAD_DIM), kn, vn, kc, vc,
                         bias_dec, bias_own, sink_tab).reshape(n_dec, SWA_Q_W)
    w_out = w_out_c[0].astype(BF16)
    x1_p = _outproj_ln(x, attn_p, 0, attn_p, 1, w_out, b_out_c[0], ln_mix_g[1], ln_mix_b[1], ROW_TILE)
    x1_s = _outproj_ln(x[n_prompt:n_real], attn_s, 0, attn_s, 1, w_out, b_out_c[0], ln_mix_g[1],
                       ln_mix_b[1], n_dec)
    kv_p = proj[:n_prompt].reshape(bsz, t_len, -1)[:, t_len - WINDOW:, SWA_Q_W:]
    heads = (SWA_KV_HEADS, SWA_HEAD_DIM)
    p_k = kv_p[:, :, :SWA_KV_W].reshape(bsz, WINDOW, *heads)
    p_v = kv_p[:, :, SWA_KV_W:].reshape(bsz, WINDOW, *heads)
    s_k = jnp.concatenate([kc[:, 1:], kn[:, None]], axis=1).reshape(n_dec, WINDOW, *heads)
    s_v = jnp.concatenate([vc[:, 1:], vn[:, None]], axis=1).reshape(n_dec, WINDOW, *heads)
    return x1_p, x1_s, p_k, p_v, s_k, s_v
```

```python
import functools
import math

import jax
import jax.numpy as jnp
from jax import lax
from jax.experimental import pallas as pl
from jax.experimental.pallas import tpu as pltpu

F32 = jnp.float32
BF16 = jnp.bfloat16

D_MODEL = 1024
DEPTH = 2
DN_ALPHA = (2 * DEPTH) ** 0.25
LN_EPS = 1e-5

PEER_HEADS = 8
PEER_N_KEYS = 128
PEER_D_SUB = 128
PEER_TOPK = 16
PEER_Q_W = PEER_HEADS * 2 * PEER_D_SUB

LANES = 128
SUBLANES = 8
VMEM_LIMIT = 56 * 1024 * 1024

TOK_TILE = 256
EXP_BLOCK = 2048
MXU_PARTS = 4
NEG_INF = float("-inf")


def _oddeven_merge(lo, hi, r):
    step = r * 2
    if step < hi - lo:
        yield from _oddeven_merge(lo, hi, step)
        yield from _oddeven_merge(lo + r, hi, step)
        for i in range(lo + r, hi - r, step):
            yield (i, i + r)
    else:
        yield (lo, lo + r)


def _oddeven_sort_pairs(lo, hi):
    if hi - lo >= 1:
        mid = lo + (hi - lo) // 2
        yield from _oddeven_sort_pairs(lo, mid)
        yield from _oddeven_sort_pairs(mid + 1, hi)
        yield from _oddeven_merge(lo, hi, 1)


_SORT16 = tuple(_oddeven_sort_pairs(0, 15))


def _cmpx(v, i, j):
    a, b = v[i], v[j]
    if b is None:
        return
    if a is None:
        v[i], v[j] = b, None
        return
    v[i], v[j] = jnp.maximum(a, b), jnp.minimum(a, b)


def _sort16_desc(v):
    v = list(v)
    for i, j in _SORT16:
        _cmpx(v, i, j)
    return v


def _merge_top16(x, y):
    v = []
    for k in range(16):
        a, b = x[k], y[15 - k]
        v.append(b if a is None else a if b is None else jnp.maximum(a, b))
    for d in (8, 4, 2, 1):
        for i in range(16):
            if not i & d:
                _cmpx(v, i, i + d)
    return v


def _top16_of_keys(s_t):
    v = _sort16_desc([s_t[SUBLANES * r:SUBLANES * (r + 1), :] for r in range(16)])
    for shift in (4, 2, 1):
        v = _merge_top16(v, [pltpu.roll(a, shift, 0) for a in v])
    return v


_PAIR_ROWS = [[(k, l) for l in range(PEER_TOPK // (k + 1))] for k in range(PEER_TOPK)]


def _top16_pair_sums(a, b):
    top = [a[0] + b[l] for l in range(16)]
    rest = [a[k] + b[l] for row in _PAIR_ROWS[1:] for (k, l) in row]
    for lo in range(0, len(rest), 16):
        chunk = rest[lo:lo + 16]
        chunk = chunk + [None] * (16 - len(chunk))
        top = _merge_top16(top, _sort16_desc(chunk))
    return top


def _peer_select_kernel(x_ref, wq_ref, keys_ref, xt_ref, thr_ref, a1_ref, s2_ref, b2_ref, q_scr):
    x = x_ref[...]
    xt_ref[...] = x.T.astype(BF16)
    q = jnp.dot(x.astype(BF16), wq_ref[...], preferred_element_type=F32).astype(BF16)
    for hc in range(2 * PEER_HEADS):
        q_scr[hc] = q[:, hc * PEER_D_SUB:(hc + 1) * PEER_D_SUB]

    def head(h, carry):
        for tc in range(TOK_TILE // LANES):
            rows = slice(tc * LANES, (tc + 1) * LANES)
            nt = (((1,), (1,)), ((), ()))
            s1 = lax.dot_general(keys_ref[h, 0], q_scr[2 * h, rows, :], nt,
                                 preferred_element_type=F32)
            s2 = lax.dot_general(keys_ref[h, 1], q_scr[2 * h + 1, rows, :], nt,
                                 preferred_element_type=F32)
            a = _top16_of_keys(s1)
            b = _top16_of_keys(s2)
            top = _top16_pair_sums(a, b)
            m = a[0] + b[0]
            z = jnp.exp(top[0] - m)
            for k in range(1, 16):
                z = z + jnp.exp(top[k] - m)
            theta = top[15][0:1, :]
            inv_z = (1.0 / z)[0:1, :]
            thr_ref[h, :, rows] = theta - s1
            a1_ref[h, :, rows] = jnp.exp(s1 - a[0][0:1, :]) * inv_z
            s2_ref[h, :, rows] = s2
            b2_ref[h, :, rows] = jnp.exp(s2 - b[0][0:1, :])
        return carry

    lax.fori_loop(0, PEER_HEADS, head, 0)


def _gelu_tanh(x):
    return 0.5 * x * (1.0 + jnp.tanh(math.sqrt(2.0 / math.pi) * (x + 0.044715 * (x * x * x))))


def _peer_dense_kernel(xt_ref, thr_ref, a1_ref, s2_ref, b2_ref, u_ref, vt_ref, yt_ref, act_scr, w_scr):
    e = pl.program_id(1)
    part_rows = EXP_BLOCK // MXU_PARTS

    @pl.when(e == 0)
    def _():
        yt_ref[...] = jnp.zeros_like(yt_ref)

    def gate_block(r, tc):
        lanes = slice(tc * LANES, (tc + 1) * LANES)
        rows = slice(r * PEER_N_KEYS, (r + 1) * PEER_N_KEYS)
        w = jnp.zeros((PEER_N_KEYS, LANES), F32)
        for h in range(PEER_HEADS):
            thr = thr_ref[h, r:r + 1, lanes]
            a1 = a1_ref[h, r:r + 1, lanes]
            w = w + jnp.where(s2_ref[h, :, lanes] >= thr, b2_ref[h, :, lanes], 0.0) * a1
        w_scr[rows, lanes] = (w * _gelu_tanh(act_scr[rows, lanes])).astype(BF16)

    for part in range(MXU_PARTS):
        ra = slice(part * part_rows, (part + 1) * part_rows)
        act_scr[ra, :] = jnp.dot(u_ref[ra, :], xt_ref[...], preferred_element_type=F32)
        for r in range(part * part_rows // PEER_N_KEYS, (part + 1) * part_rows // PEER_N_KEYS):
            for tc in range(TOK_TILE // LANES):
                gate_block(r, tc)
        yt_ref[...] += jnp.dot(vt_ref[:, ra], w_scr[ra, :], preferred_element_type=F32)


def _resid_ln_t_kernel(x_ref, yt_ref, g_ref, b_ref, o_ref):
    z = DN_ALPHA * x_ref[...] + yt_ref[...].T
    mu = jnp.mean(z, axis=-1, keepdims=True)
    zc = z - mu
    var = jnp.mean(zc * zc, axis=-1, keepdims=True)
    o_ref[...] = zc * lax.rsqrt(var + LN_EPS) * g_ref[...] + b_ref[...]


def _peer_layer(x, wq, keys, u, vt, ln_g, ln_b):
    n_tok = x.shape[0]
    n_tiles = n_tok // TOK_TILE
    n_exp = u.shape[0]
    fac = jax.ShapeDtypeStruct((PEER_HEADS, PEER_N_KEYS, n_tok), F32)
    fac_spec = pl.BlockSpec((PEER_HEADS, PEER_N_KEYS, TOK_TILE), lambda t: (0, 0, t))
    xt, thr, a1, s2, b2 = pl.pallas_call(
        _peer_select_kernel,
        grid=(n_tiles,),
        in_specs=[
            pl.BlockSpec((TOK_TILE, D_MODEL), lambda t: (t, 0)),
            pl.BlockSpec((D_MODEL, PEER_Q_W), lambda t: (0, 0)),
            pl.BlockSpec((PEER_HEADS, 2, PEER_N_KEYS, PEER_D_SUB), lambda t: (0, 0, 0, 0)),
        ],
        out_specs=[pl.BlockSpec((D_MODEL, TOK_TILE), lambda t: (0, t))] + [fac_spec] * 4,
        out_shape=[jax.ShapeDtypeStruct((D_MODEL, n_tok), BF16)] + [fac] * 4,
        scratch_shapes=[pltpu.VMEM((2 * PEER_HEADS, TOK_TILE, PEER_D_SUB), BF16)],
        compiler_params=pltpu.CompilerParams(dimension_semantics=("parallel",),
                                             vmem_limit_bytes=VMEM_LIMIT),
        name="peer_select",
    )(x, wq, keys)

    fac_spec2 = pl.BlockSpec((PEER_HEADS, PEER_N_KEYS, TOK_TILE), lambda t, e: (0, 0, t))
    row_spec = pl.BlockSpec((PEER_HEADS, EXP_BLOCK // PEER_N_KEYS, TOK_TILE), lambda t, e: (0, e, t))
    yt = pl.pallas_call(
        _peer_dense_kernel,
        grid=(n_tiles, n_exp // EXP_BLOCK),
        in_specs=[pl.BlockSpec((D_MODEL, TOK_TILE), lambda t, e: (0, t)),
                  row_spec, row_spec, fac_spec2, fac_spec2] + [
            pl.BlockSpec((EXP_BLOCK, D_MODEL), lambda t, e: (e, 0)),
            pl.BlockSpec((D_MODEL, EXP_BLOCK), lambda t, e: (0, e)),
        ],
        out_specs=pl.BlockSpec((D_MODEL, TOK_TILE), lambda t, e: (0, t)),
        out_shape=jax.ShapeDtypeStruct((D_MODEL, n_tok), F32),
        scratch_shapes=[pltpu.VMEM((EXP_BLOCK, TOK_TILE), F32), pltpu.VMEM((EXP_BLOCK, TOK_TILE), BF16)],
        compiler_params=pltpu.CompilerParams(dimension_semantics=("parallel", "arbitrary"),
                                             vmem_limit_bytes=VMEM_LIMIT),
        name="peer_dense",
    )(xt, thr, a1, s2, b2, u, vt)

    return pl.pallas_call(
        _resid_ln_t_kernel,
        grid=(n_tiles,),
        in_specs=[
            pl.BlockSpec((TOK_TILE, D_MODEL), lambda t: (t, 0)),
            pl.BlockSpec((D_MODEL, TOK_TILE), lambda t: (0, t)),
            pl.BlockSpec((1, D_MODEL), lambda t: (0, 0)),
            pl.BlockSpec((1, D_MODEL), lambda t: (0, 0)),
        ],
        out_specs=pl.BlockSpec((TOK_TILE, D_MODEL), lambda t: (t, 0)),
        out_shape=jax.ShapeDtypeStruct((n_tok, D_MODEL), F32),
        compiler_params=pltpu.CompilerParams(dimension_semantics=("parallel",)),
        name="peer_resid_ln",
    )(x, yt, ln_g.reshape(1, D_MODEL), ln_b.reshape(1, D_MODEL))


GDN_HEADS = 4
GDN_DK = 128
GDN_DV = 128
GDN_CHUNK = 64
CONV_W = 4
GDN_QK_W = GDN_HEADS * GDN_DK
GDN_V_W = GDN_HEADS * GDN_DV
GDN_CONV_CH = 2 * GDN_QK_W + GDN_V_W
LRU_WIDTH = 512
LRU_BLOCKS = 8
LRU_C = 8.0
SWA_HEADS = 16
SWA_KV_HEADS = 4
SWA_HEAD_DIM = 64
SWA_GROUP = SWA_HEADS // SWA_KV_HEADS
SWA_Q_W = SWA_HEADS * SWA_HEAD_DIM
SWA_KV_W = SWA_KV_HEADS * SWA_HEAD_DIM
WINDOW = 128
REL_BUCKETS = 32
REL_MAX_DIST = 128
MASKED = -1e30

SEQ_TILE = 256
ROW_TILE = 256
DEC_TILE = 8

_NT = (((1,), (1,)), ((), ()))
_TN = (((0,), (0,)), ((), ()))


def _mm(a, b):
    return jnp.dot(a.astype(BF16), b.astype(BF16), preferred_element_type=F32)


def _mm_nt(a, b):
    return lax.dot_general(a.astype(BF16), b.astype(BF16), _NT, preferred_element_type=F32)


def _mm_tn(a, b):
    return lax.dot_general(a.astype(BF16), b.astype(BF16), _TN, preferred_element_type=F32)


def _mm_f32(a, b):
    return jnp.dot(a, b, preferred_element_type=F32, precision=lax.Precision.HIGHEST)


def _sigmoid(x):
    return 1.0 / (1.0 + jnp.exp(-x))


def _silu(x):
    return x * _sigmoid(x)


def _softplus(x):
    return jnp.maximum(x, 0.0) + jnp.log1p(jnp.exp(-jnp.abs(x)))


def _layer_norm(z, g, b):
    mu = jnp.mean(z, axis=-1, keepdims=True)
    zc = z - mu
    var = jnp.mean(zc * zc, axis=-1, keepdims=True)
    return zc * lax.rsqrt(var + LN_EPS) * g + b


def _l2_normalize(x):
    return x * lax.rsqrt(jnp.sum(x * x, axis=-1, keepdims=True) + 1e-6)


def _causal_conv(x, xp_scr, w_ref, n_rows):
    xp_scr[8:8 + n_rows, :] = x
    y = x * w_ref[CONV_W - 1:CONV_W, :]
    for i in range(CONV_W - 1):
        y = y + xp_scr[5 + i:5 + i + n_rows, :] * w_ref[i:i + 1, :]
    xp_scr[0:8, :] = x[n_rows - 8:n_rows, :]
    return y


def _gdn_gates(ab, alog_row, dtb_row):
    g = -jnp.exp(alog_row) * _softplus(ab + dtb_row)
    return g, _sigmoid(ab)


def _gdn_out_norm(o, z, nw_row):
    o = o * lax.rsqrt(jnp.mean(o * o, axis=-1, keepdims=True) + 1e-6) * nw_row
    return o * _silu(z)


def _lru_coeffs(xc, wr_ref, br_ref, wi_ref, bi_ref, lam_ref):
    r = _sigmoid(_mm(xc, wr_ref[...]) + br_ref[...])
    i = _sigmoid(_mm(xc, wi_ref[...]) + bi_ref[...])
    log_a = -LRU_C * r * _softplus(-lam_ref[...])
    a = jnp.exp(log_a)
    one_minus_a2 = -jnp.tanh(log_a) * (a * a + 1.0)
    return a, jnp.sqrt(one_minus_a2) * (i * xc)


def _proj_kernel(x_ref, w_ref, b_ref, o_ref):
    o_ref[...] = _mm(x_ref[...], w_ref[...]) + b_ref[...]


def _project(x, w, b):
    n, k = x.shape
    m = w.shape[1]
    return pl.pallas_call(
        _proj_kernel,
        grid=(n // ROW_TILE,),
        in_specs=[pl.BlockSpec((ROW_TILE, k), lambda t: (t, 0)),
                  pl.BlockSpec((k, m), lambda t: (0, 0)),
                  pl.BlockSpec((1, m), lambda t: (0, 0))],
        out_specs=pl.BlockSpec((ROW_TILE, m), lambda t: (t, 0)),
        out_shape=jax.ShapeDtypeStruct((n, m), F32),
        compiler_params=pltpu.CompilerParams(dimension_semantics=("parallel",),
                                             vmem_limit_bytes=VMEM_LIMIT),
        name="proj",
    )(x, w, b.reshape(1, m))


def _outproj_ln_kernel(x_ref, oa_ref, ob_ref, w_ref, b_ref, g_ref, beta_ref, o_ref):
    half = oa_ref.shape[1]
    y = _mm(oa_ref[...], w_ref[0:half, :]) + _mm(ob_ref[...], w_ref[half:2 * half, :]) + b_ref[...]
    o_ref[...] = _layer_norm(DN_ALPHA * x_ref[...] + y, g_ref[...], beta_ref[...])


def _outproj_ln(x, oa, oa_col, ob, ob_col, w, b, g, beta, tile):
    n = oa.shape[0]
    half = w.shape[0] // 2
    row = lambda t: (0, 0)
    return pl.pallas_call(
        _outproj_ln_kernel,
        grid=(n // tile,),
        in_specs=[pl.BlockSpec((tile, D_MODEL), lambda t: (t, 0)),
                  pl.BlockSpec((tile, half), lambda t: (t, oa_col)),
                  pl.BlockSpec((tile, half), lambda t: (t, ob_col)),
                  pl.BlockSpec((2 * half, D_MODEL), row),
                  pl.BlockSpec((1, D_MODEL), row), pl.BlockSpec((1, D_MODEL), row),
                  pl.BlockSpec((1, D_MODEL), row)],
        out_specs=pl.BlockSpec((tile, D_MODEL), lambda t: (t, 0)),
        out_shape=jax.ShapeDtypeStruct((n, D_MODEL), F32),
        compiler_params=pltpu.CompilerParams(dimension_semantics=("parallel",),
                                             vmem_limit_bytes=VMEM_LIMIT),
        name="outproj_ln",
    )(x, oa, ob, w, b.reshape(1, D_MODEL), g.reshape(1, D_MODEL), beta.reshape(1, D_MODEL))


def _gdn_prompt_kernel(qkv_ref, ab_ref, z_ref, cw_ref, alog_ref, dtb_ref, nw_ref, o_ref, s_out_ref,
                       xp_scr, s_scr, q_scr, k_scr, v_scr, gc_scr, beta_scr):
    t = pl.program_id(1)

    @pl.when(t == 0)
    def _():
        xp_scr[0:8, :] = jnp.zeros((8, GDN_CONV_CH), F32)
        s_scr[...] = jnp.zeros_like(s_scr)

    y = _silu(_causal_conv(qkv_ref[...], xp_scr, cw_ref, SEQ_TILE))
    for h in range(GDN_HEADS):
        cols = slice(h * GDN_DK, (h + 1) * GDN_DK)
        q_scr[:, cols] = _l2_normalize(y[:, cols]) * (GDN_DK ** -0.5)
        k_scr[:, cols] = _l2_normalize(y[:, GDN_QK_W + h * GDN_DK:GDN_QK_W + (h + 1) * GDN_DK])
    v_scr[...] = y[:, 2 * GDN_QK_W:]
    g, beta = _gdn_gates(ab_ref[...], alog_ref[...], dtb_ref[...])
    beta_scr[...] = beta
    pos = lax.broadcasted_iota(jnp.int32, g.shape, 0) % GDN_CHUNK
    shift = 1
    while shift < GDN_CHUNK:
        g = g + jnp.where(pos >= shift, pltpu.roll(g, shift, 0), 0.0)
        shift *= 2
    gc_scr[...] = g

    ri = lax.broadcasted_iota(jnp.int32, (GDN_CHUNK, GDN_CHUNK), 0)
    ci = lax.broadcasted_iota(jnp.int32, (GDN_CHUNK, GDN_CHUNK), 1)
    eye = ri == ci
    incl = ri >= ci
    strict = ri > ci

    def chunk(c, carry):
        rows = pl.ds(pl.multiple_of(c * GDN_CHUNK, GDN_CHUNK), GDN_CHUNK)
        gc_all = gc_scr[rows, :]
        beta_all = beta_scr[rows, :]
        for h in range(GDN_HEADS):
            cols = slice(h * GDN_DK, (h + 1) * GDN_DK)
            q = q_scr[rows, cols]
            k = k_scr[rows, cols]
            v = v_scr[rows, cols]
            gcol = gc_all[:, h:h + 1]
            bcol = beta_all[:, GDN_HEADS + h:GDN_HEADS + h + 1]
            grow = jnp.sum(jnp.where(eye, gcol, 0.0), axis=0, keepdims=True)
            decay = jnp.where(incl, jnp.exp(jnp.where(incl, gcol - grow, 0.0)), 0.0)
            kb = k * bcol
            neg = -jnp.where(strict, _mm_nt(kb, k) * decay, 0.0)
            inv = jnp.where(eye, 1.0, 0.0) + neg
            pw = neg
            for _ in range(5):
                pw = _mm_f32(pw, pw)
                inv = inv + _mm_f32(inv, pw)
            egc = jnp.exp(gcol)
            rhs = jnp.concatenate([v * bcol, kb * egc], axis=1)
            sol = _mm_f32(inv, rhs)
            u, w = sol[:, :GDN_DV], sol[:, GDN_DV:]
            qk = jnp.where(incl, _mm_nt(q, k) * decay, 0.0)
            s = s_scr[h]
            v_new = u - _mm(w, s)
            o = _mm(q * egc, s) + _mm(qk, v_new)
            g_last = gcol[GDN_CHUNK - 1:GDN_CHUNK, :]
            s_scr[h] = s * jnp.exp(g_last) + _mm_tn(k * jnp.exp(g_last - gcol), v_new)
            o_ref[rows, cols] = _gdn_out_norm(o, z_ref[rows, cols], nw_ref[...])
        return carry

    lax.fori_loop(0, SEQ_TILE // GDN_CHUNK, chunk, 0)

    @pl.when(t == pl.num_programs(1) - 1)
    def _():
        s_out_ref[0] = s_scr[...]


def _gdn_prompt(proj, bsz, t_len, cw, alog_row, dtb_row, nw_row):
    n_t = t_len // SEQ_TILE
    row = lambda b, t: (0, 0)
    return pl.pallas_call(
        _gdn_prompt_kernel,
        grid=(bsz, n_t),
        in_specs=[pl.BlockSpec((SEQ_TILE, GDN_CONV_CH), lambda b, t: (b * n_t + t, 0)),
                  pl.BlockSpec((SEQ_TILE, LANES), lambda b, t: (b * n_t + t, 24)),
                  pl.BlockSpec((SEQ_TILE, GDN_V_W), lambda b, t: (b * n_t + t, 3)),
                  pl.BlockSpec((CONV_W, GDN_CONV_CH), row),
                  pl.BlockSpec((1, LANES), row), pl.BlockSpec((1, LANES), row),
                  pl.BlockSpec((1, GDN_DV), row)],
        out_specs=[pl.BlockSpec((SEQ_TILE, GDN_V_W), lambda b, t: (b * n_t + t, 0)),
                   pl.BlockSpec((1, GDN_HEADS, GDN_DK, GDN_DV), lambda b, t: (b, 0, 0, 0))],
        out_shape=[jax.ShapeDtypeStruct((bsz * t_len, GDN_V_W), F32),
                   jax.ShapeDtypeStruct((bsz, GDN_HEADS, GDN_DK, GDN_DV), F32)],
        scratch_shapes=[pltpu.VMEM((SEQ_TILE + 8, GDN_CONV_CH), F32),
                        pltpu.VMEM((GDN_HEADS, GDN_DK, GDN_DV), F32),
                        pltpu.VMEM((SEQ_TILE, GDN_QK_W), F32), pltpu.VMEM((SEQ_TILE, GDN_QK_W), F32),
                        pltpu.VMEM((SEQ_TILE, GDN_V_W), F32),
                        pltpu.VMEM((SEQ_TILE, LANES), F32), pltpu.VMEM((SEQ_TILE, LANES), F32)],
        compiler_params=pltpu.CompilerParams(dimension_semantics=("parallel", "arbitrary"),
                                             vmem_limit_bytes=VMEM_LIMIT),
        name="gdn_prompt",
    )(proj, proj, proj, cw, alog_row, dtb_row, nw_row)


def _lru_prompt_kernel(xr_ref, gate_ref, cw_ref, cb_ref, wr_ref, br_ref, wi_ref, bi_ref, lam_ref,
                       o_ref, h_out_ref, xp_scr, h_scr):
    t = pl.program_id(1)

    @pl.when(t == 0)
    def _():
        xp_scr[0:8, :] = jnp.zeros((8, LRU_WIDTH), F32)
        h_scr[...] = jnp.zeros_like(h_scr)

    xc = _causal_conv(xr_ref[...], xp_scr, cw_ref, SEQ_TILE) + cb_ref[...]
    a, b = _lru_coeffs(xc, wr_ref, br_ref, wi_ref, bi_ref, lam_ref)
    pos = lax.broadcasted_iota(jnp.int32, a.shape, 0)
    shift = 1
    while shift < SEQ_TILE:
        valid = pos >= shift
        b = jnp.where(valid, a * pltpu.roll(b, shift, 0) + b, b)
        a = jnp.where(valid, a * pltpu.roll(a, shift, 0), a)
        shift *= 2
    h = b + a * h_scr[...]
    h_scr[...] = h[SEQ_TILE - 1:SEQ_TILE, :]
    o_ref[...] = jax.nn.gelu(gate_ref[...]) * h

    @pl.when(t == pl.num_programs(1) - 1)
    def _():
        h_out_ref[0] = h[SEQ_TILE - 1:SEQ_TILE, :]


def _lru_prompt(proj, bsz, t_len, lru_w):
    n_t = t_len // SEQ_TILE
    row = lambda b, t: (0, 0)
    wide = pl.BlockSpec((1, LRU_WIDTH), row)
    sq = pl.BlockSpec((LRU_WIDTH, LRU_WIDTH), row)
    return pl.pallas_call(
        _lru_prompt_kernel,
        grid=(bsz, n_t),
        in_specs=[pl.BlockSpec((SEQ_TILE, LRU_WIDTH), lambda b, t: (b * n_t + t, 4)),
                  pl.BlockSpec((SEQ_TILE, LRU_WIDTH), lambda b, t: (b * n_t + t, 5)),
                  pl.BlockSpec((CONV_W, LRU_WIDTH), row), wide, sq, wide, sq, wide, wide],
        out_specs=[pl.BlockSpec((SEQ_TILE, LRU_WIDTH), lambda b, t: (b * n_t + t, 0)),
                   pl.BlockSpec((1, 1, LRU_WIDTH), lambda b, t: (b, 0, 0))],
        out_shape=[jax.ShapeDtypeStruct((bsz * t_len, LRU_WIDTH), F32),
                   jax.ShapeDtypeStruct((bsz, 1, LRU_WIDTH), F32)],
        scratch_shapes=[pltpu.VMEM((SEQ_TILE + 8, LRU_WIDTH), F32), pltpu.VMEM((1, LRU_WIDTH), F32)],
        compiler_params=pltpu.CompilerParams(dimension_semantics=("parallel", "arbitrary"),
                                             vmem_limit_bytes=VMEM_LIMIT),
        name="lru_prompt",
    )(proj, proj, *lru_w)


def _ab_sample_kernel(qkv_ref, ab_ref, z_ref, xr_ref, gate_ref, gbuf_ref, lbuf_ref, s_ref, h0_ref,
                      cw_ref, alog_ref, dtb_ref, nw_ref,
                      lcw_ref, lcb_ref, wr_ref, br_ref, wi_ref, bi_ref, lam_ref,
                      oa_ref, ob_ref, s_out_ref, h_out_ref, o_scr):
    y = qkv_ref[...] * cw_ref[CONV_W - 1:CONV_W, :]
    for i in range(CONV_W - 1):
        y = y + gbuf_ref[i] * cw_ref[i:i + 1, :]
    y = _silu(y)
    g, beta = _gdn_gates(ab_ref[...], alog_ref[...], dtb_ref[...])
    eg = jnp.exp(g)
    for h in range(GDN_HEADS):
        cols = slice(h * GDN_DK, (h + 1) * GDN_DK)
        q = _l2_normalize(y[:, cols]) * (GDN_DK ** -0.5)
        k = _l2_normalize(y[:, GDN_QK_W + h * GDN_DK:GDN_QK_W + (h + 1) * GDN_DK])
        v = y[:, 2 * GDN_QK_W + h * GDN_DV:2 * GDN_QK_W + (h + 1) * GDN_DV]
        qk = jnp.sum(q * k, axis=-1, keepdims=True)
        q_t = q.T
        k_t = k.T
        for b in range(DEC_TILE):
            s = s_ref[b, h]
            kcol = k_t[:, b:b + 1]
            e = eg[b:b + 1, h:h + 1]
            ks = jnp.sum(s * kcol, axis=0, keepdims=True)
            qs = jnp.sum(s * q_t[:, b:b + 1], axis=0, keepdims=True)
            v_new = beta[b:b + 1, GDN_HEADS + h:GDN_HEADS + h + 1] * (v[b:b + 1, :] - e * ks)
            o_scr[b:b + 1, cols] = e * qs + qk[b:b + 1, :] * v_new
            s_out_ref[b, h] = e * s + kcol * v_new
    for h in range(GDN_HEADS):
        cols = slice(h * GDN_DV, (h + 1) * GDN_DV)
        oa_ref[:, cols] = _gdn_out_norm(o_scr[:, cols], z_ref[:, cols], nw_ref[...])
    xr = xr_ref[...]
    xc = xr * lcw_ref[CONV_W - 1:CONV_W, :] + lcb_ref[...]
    for i in range(CONV_W - 1):
        xc = xc + lbuf_ref[i] * lcw_ref[i:i + 1, :]
    a, bb = _lru_coeffs(xc, wr_ref, br_ref, wi_ref, bi_ref, lam_ref)
    hid = a * h0_ref[...] + bb
    h_out_ref[...] = hid
    ob_ref[...] = jax.nn.gelu(gate_ref[...]) * hid


def _ab_sample(proj, row0, n_dec, gbuf_t, lbuf_t, s0, h0, gdn_w, lru_w):
    blk0 = row0 // DEC_TILE
    row = lambda i: (0, 0)
    wide = pl.BlockSpec((1, LRU_WIDTH), row)
    sq = pl.BlockSpec((LRU_WIDTH, LRU_WIDTH), row)
    return pl.pallas_call(
        _ab_sample_kernel,
        grid=(n_dec // DEC_TILE,),
        in_specs=[pl.BlockSpec((DEC_TILE, GDN_CONV_CH), lambda i: (blk0 + i, 0)),
                  pl.BlockSpec((DEC_TILE, LANES), lambda i: (blk0 + i, 24)),
                  pl.BlockSpec((DEC_TILE, GDN_V_W), lambda i: (blk0 + i, 3)),
                  pl.BlockSpec((DEC_TILE, LRU_WIDTH), lambda i: (blk0 + i, 4)),
                  pl.BlockSpec((DEC_TILE, LRU_WIDTH), lambda i: (blk0 + i, 5)),
                  pl.BlockSpec((CONV_W - 1, DEC_TILE, GDN_CONV_CH), lambda i: (0, i, 0)),
                  pl.BlockSpec((CONV_W - 1, DEC_TILE, LRU_WIDTH), lambda i: (0, i, 0)),
                  pl.BlockSpec((DEC_TILE, GDN_HEADS, GDN_DK, GDN_DV), lambda i: (i, 0, 0, 0)),
                  pl.BlockSpec((DEC_TILE, LRU_WIDTH), lambda i: (i, 0)),
                  pl.BlockSpec((CONV_W, GDN_CONV_CH), row),
                  pl.BlockSpec((1, LANES), row), pl.BlockSpec((1, LANES), row),
                  pl.BlockSpec((1, GDN_DV), row),
                  pl.BlockSpec((CONV_W, LRU_WIDTH), row), wide, sq, wide, sq, wide, wide],
        out_specs=[pl.BlockSpec((DEC_TILE, GDN_V_W), lambda i: (i, 0)),
                   pl.BlockSpec((DEC_TILE, LRU_WIDTH), lambda i: (i, 0)),
                   pl.BlockSpec((DEC_TILE, GDN_HEADS, GDN_DK, GDN_DV), lambda i: (i, 0, 0, 0)),
                   pl.BlockSpec((DEC_TILE, LRU_WIDTH), lambda i: (i, 0))],
        out_shape=[jax.ShapeDtypeStruct((n_dec, GDN_V_W), F32),
                   jax.ShapeDtypeStruct((n_dec, LRU_WIDTH), F32),
                   jax.ShapeDtypeStruct((n_dec, GDN_HEADS, GDN_DK, GDN_DV), F32),
                   jax.ShapeDtypeStruct((n_dec, LRU_WIDTH), F32)],
        scratch_shapes=[pltpu.VMEM((DEC_TILE, GDN_V_W), F32)],
        compiler_params=pltpu.CompilerParams(dimension_semantics=("parallel",),
                                             vmem_limit_bytes=VMEM_LIMIT),
        name="ab_sample",
    )(proj, proj, proj, proj, proj, gbuf_t, lbuf_t, s0, h0, *gdn_w, *lru_w)


def _softmax_sink_pv(logits, sink, v, extra_logit=None, extra_v=None):
    m = jnp.maximum(jnp.max(logits, axis=-1, keepdims=True), sink)
    if extra_logit is not None:
        m = jnp.maximum(m, extra_logit)
    p = jnp.exp(logits - m)
    den = jnp.sum(p, axis=-1, keepdims=True) + jnp.exp(sink - m)
    acc = _mm(p, v)
    if extra_logit is not None:
        pe = jnp.exp(extra_logit - m)
        den = den + pe
        acc = acc + pe.astype(BF16).astype(F32) * extra_v.astype(BF16).astype(F32)
    return acc / den


def _swa_prompt_kernel(q_ref, kc_ref, kp_ref, vc_ref, vp_ref, bias_ref, sink_ref, o_ref):
    n = pl.program_id(1)
    kcat = jnp.concatenate([kp_ref[...], kc_ref[...]], axis=0)
    vcat = jnp.concatenate([vp_ref[...], vc_ref[...]], axis=0)
    qi = lax.broadcasted_iota(jnp.int32, (WINDOW, 2 * WINDOW), 0)
    ki = lax.broadcasted_iota(jnp.int32, (WINDOW, 2 * WINDOW), 1)
    rel = qi + WINDOW - ki
    mask = (rel >= 0) & (rel < WINDOW) & ((ki >= WINDOW) | (n > 0))
    for h in range(SWA_HEADS):
        g = h // SWA_GROUP
        kv_cols = slice(g * SWA_HEAD_DIM, (g + 1) * SWA_HEAD_DIM)
        q = q_ref[:, h * SWA_HEAD_DIM:(h + 1) * SWA_HEAD_DIM]
        logits = _mm_nt(q, kcat[:, kv_cols]) * (SWA_HEAD_DIM ** -0.5) + bias_ref[h]
        logits = jnp.where(mask, logits, MASKED)
        o_ref[:, h * SWA_HEAD_DIM:(h + 1) * SWA_HEAD_DIM] = _softmax_sink_pv(
            logits, sink_ref[h:h + 1, 0:1], vcat[:, kv_cols])


def _swa_prompt(proj, bsz, t_len, bias_tab, sink_tab):
    n_blk = t_len // WINDOW
    cur = lambda col: (lambda b, n: (b * n_blk + n, col))
    prev = lambda col: (lambda b, n: (b * n_blk + jnp.maximum(n - 1, 0), col))
    return pl.pallas_call(
        _swa_prompt_kernel,
        grid=(bsz, n_blk),
        in_specs=[pl.BlockSpec((WINDOW, SWA_Q_W), cur(0)),
                  pl.BlockSpec((WINDOW, SWA_KV_W), cur(4)), pl.BlockSpec((WINDOW, SWA_KV_W), prev(4)),
                  pl.BlockSpec((WINDOW, SWA_KV_W), cur(5)), pl.BlockSpec((WINDOW, SWA_KV_W), prev(5)),
                  pl.BlockSpec((SWA_HEADS, WINDOW, 2 * WINDOW), lambda b, n: (0, 0, 0)),
                  pl.BlockSpec((SWA_HEADS, LANES), lambda b, n: (0, 0))],
        out_specs=pl.BlockSpec((WINDOW, SWA_Q_W), lambda b, n: (b * n_blk + n, 0)),
        out_shape=jax.ShapeDtypeStruct((bsz * t_len, SWA_Q_W), F32),
        compiler_params=pltpu.CompilerParams(dimension_semantics=("parallel", "arbitrary"),
                                             vmem_limit_bytes=VMEM_LIMIT),
        name="swa_prompt",
    )(proj, proj, proj, proj, proj, bias_tab, sink_tab)


def _swa_sample_kernel(q_ref, kn_ref, vn_ref, kc_ref, vc_ref, bias_ref, bias0_ref, sink_ref, o_ref):
    lane = lax.broadcasted_iota(jnp.int32, (SWA_GROUP, WINDOW), 1)
    for b in range(DEC_TILE):
        qb = q_ref[b]
        for g in range(SWA_KV_HEADS):
            hs = slice(g * SWA_GROUP, (g + 1) * SWA_GROUP)
            kv_cols = slice(g * SWA_HEAD_DIM, (g + 1) * SWA_HEAD_DIM)
            q = qb[hs, :]
            logits = _mm_nt(q, kc_ref[b, :, kv_cols]) * (SWA_HEAD_DIM ** -0.5) + bias_ref[hs, :]
            logits = jnp.where(lane >= 1, logits, MASKED)
            kn = kn_ref[b:b + 1, kv_cols]
            own = jnp.sum(q.astype(BF16).astype(F32) * kn.astype(BF16).astype(F32), axis=-1,
                          keepdims=True) * (SWA_HEAD_DIM ** -0.5) + bias0_ref[hs, 0:1]
            o_ref[b, hs, :] = _softmax_sink_pv(logits, sink_ref[hs, 0:1], vc_ref[b, :, kv_cols],
                                               own, vn_ref[b:b + 1, kv_cols])


def _swa_sample(q3, kn, vn, kc, vc, bias_dec, bias0, sink_tab):
    n_dec = q3.shape[0]
    tab = lambda i: (0, 0)
    return pl.pallas_call(
        _swa_sample_kernel,
        grid=(n_dec // DEC_TILE,),
        in_specs=[pl.BlockSpec((DEC_TILE, SWA_HEADS, SWA_HEAD_DIM), lambda i: (i, 0, 0)),
                  pl.BlockSpec((DEC_TILE, SWA_KV_W), lambda i: (i, 0)),
                  pl.BlockSpec((DEC_TILE, SWA_KV_W), lambda i: (i, 0)),
                  pl.BlockSpec((DEC_TILE, WINDOW, SWA_KV_W), lambda i: (i, 0, 0)),
                  pl.BlockSpec((DEC_TILE, WINDOW, SWA_KV_W), lambda i: (i, 0, 0)),
                  pl.BlockSpec((SWA_HEADS, WINDOW), tab), pl.BlockSpec((SWA_HEADS, LANES), tab),
                  pl.BlockSpec((SWA_HEADS, LANES), tab)],
        out_specs=pl.BlockSpec((DEC_TILE, SWA_HEADS, SWA_HEAD_DIM), lambda i: (i, 0, 0)),
        out_shape=jax.ShapeDtypeStruct((n_dec, SWA_HEADS, SWA_HEAD_DIM), F32),
        compiler_params=pltpu.CompilerParams(dimension_semantics=("parallel",),
                                             vmem_limit_bytes=VMEM_LIMIT),
        name="swa_sample",
    )(q3, kn, vn, kc, vc, bias_dec, bias0, sink_tab)


def _t5_bucket(rel):
    exact = REL_BUCKETS // 2
    nf = jnp.maximum(rel, 1).astype(F32)
    large = exact + (jnp.log(nf / exact) / math.log(REL_MAX_DIST / exact)
                     * (REL_BUCKETS - exact)).astype(jnp.int32)
    return jnp.where(rel < exact, rel, jnp.minimum(large, REL_BUCKETS - 1))


def _lane_row(v, width=LANES):
    return jnp.zeros((1, width), F32).at[0, :v.shape[0]].set(v.astype(F32))


def kernel(x_prompt, x_sample, state_gdn, state_gdn_conv, state_lru, state_lru_conv, cache_swa_k, cache_swa_v, w_in_ab, gdn_conv_w, gdn_a_log, gdn_dt_bias, gdn_norm_w, lru_conv_w, lru_conv_b, lru_w_r, lru_b_r, lru_w_i, lru_b_i, lru_lam, w_out_ab, w_in_c, b_in_c, swa_sinks, w_out_c, b_out_c, rel_bias, ln_mix_g, ln_mix_b, ln_ffn_g, ln_ffn_b, peer_w_q, peer_keys, peer_u, peer_v):
    bsz, t_len, _ = x_prompt.shape
    n_dec = x_sample.shape[0]
    n_prompt = bsz * t_len
    assert x_sample.shape[1] == 1 and cache_swa_k.shape[2] == WINDOW

    def peer_ffn(x, layer):
        return _peer_layer(x, peer_w_q[layer].astype(BF16), peer_keys[layer].astype(BF16),
                           peer_u[layer].astype(BF16), peer_v[layer].T.astype(BF16),
                           ln_ffn_g[layer], ln_ffn_b[layer])

    x = _tokens(x_prompt.reshape(n_prompt, D_MODEL), x_sample.reshape(n_dec, D_MODEL))
    (x1_p, x1_s, p_gdn, p_gdn_conv, p_lru, p_lru_conv, s_gdn, s_gdn_conv, s_lru,
     s_lru_conv) = _layer0_mixers(
        x, bsz, t_len, n_dec, state_gdn, state_gdn_conv, state_lru, state_lru_conv, w_in_ab,
        gdn_conv_w, gdn_a_log, gdn_dt_bias, gdn_norm_w, lru_conv_w, lru_conv_b, lru_w_r, lru_b_r,
        lru_w_i, lru_b_i, lru_lam, w_out_ab, ln_mix_g, ln_mix_b)
    x = peer_ffn(_tokens(x1_p, x1_s), 0)
    x1_p, x1_s, p_k, p_v, s_k, s_v = _layer1_mixers(
        x, bsz, t_len, n_dec, cache_swa_k, cache_swa_v, w_in_c, b_in_c, swa_sinks, w_out_c, b_out_c,
        rel_bias, ln_mix_g, ln_mix_b)
    x = peer_ffn(_tokens(x1_p, x1_s), 1)

    n_real = n_prompt + n_dec
    lead = lambda a: a[None]
    return (x[:n_prompt].reshape(bsz, t_len, D_MODEL), x[n_prompt:n_real].reshape(n_dec, 1, D_MODEL),
            lead(p_gdn), lead(p_gdn_conv), lead(p_lru.reshape(bsz, LRU_WIDTH)), lead(p_lru_conv),
            lead(p_k), lead(p_v),
            lead(s_gdn), lead(s_gdn_conv), lead(s_lru), lead(s_lru_conv), lead(s_k), lead(s_v))


def _tokens(xp_rows, xs_rows):
    n_real = xp_rows.shape[0] + xs_rows.shape[0]
    n_tok = -(-n_real // TOK_TILE) * TOK_TILE
    return jnp.concatenate([xp_rows, xs_rows, jnp.zeros((n_tok - n_real, D_MODEL), F32)])


def _layer0_mixers(x, bsz, t_len, n_dec, state_gdn, state_gdn_conv, state_lru, state_lru_conv,
                   w_in_ab, gdn_conv_w, gdn_a_log, gdn_dt_bias, gdn_norm_w, lru_conv_w, lru_conv_b,
                   lru_w_r, lru_b_r, lru_w_i, lru_b_i, lru_lam, w_out_ab, ln_mix_g, ln_mix_b):
    n_prompt = bsz * t_len
    n_real = n_prompt + n_dec
    assert t_len % SEQ_TILE == 0 and n_dec % DEC_TILE == 0 and n_prompt % ROW_TILE == 0
    w_in = w_in_ab[0]
    c0 = GDN_CONV_CH + GDN_V_W
    c1 = c0 + 2 * GDN_HEADS
    w_all = jnp.concatenate([w_in[:, :c0], w_in[:, c1:], w_in[:, c0:c1],
                             jnp.zeros((D_MODEL, LANES - 2 * GDN_HEADS), F32)], axis=1).astype(BF16)
    proj = _project(x, w_all, jnp.zeros((w_all.shape[1],), F32))
    gdn_w = (gdn_conv_w[0], _lane_row(gdn_a_log[0]), _lane_row(gdn_dt_bias[0]),
             gdn_norm_w[0].reshape(1, GDN_DV))
    eye_b = jnp.eye(LRU_BLOCKS, dtype=F32)

    def block_diag(w):
        return (eye_b[:, None, :, None] * w[:, :, None, :]).reshape(LRU_WIDTH, LRU_WIDTH).astype(BF16)

    wide = lambda v: v.reshape(1, LRU_WIDTH)
    lru_w = (lru_conv_w[0], wide(lru_conv_b[0]), block_diag(lru_w_r[0]), wide(lru_b_r[0]),
             block_diag(lru_w_i[0]), wide(lru_b_i[0]), wide(lru_lam[0]))
    oa_p, p_gdn = _gdn_prompt(proj, bsz, t_len, *gdn_w)
    ob_p, p_lru = _lru_prompt(proj, bsz, t_len, lru_w)
    oa_s, ob_s, s_gdn, s_lru = _ab_sample(
        proj, n_prompt, n_dec, jnp.swapaxes(state_gdn_conv[0], 0, 1),
        jnp.swapaxes(state_lru_conv[0], 0, 1), state_gdn[0], state_lru[0], gdn_w, lru_w)
    w_out = w_out_ab[0].astype(BF16)
    zero_b = jnp.zeros((D_MODEL,), F32)
    x1_p = _outproj_ln(x, oa_p, 0, ob_p, 0, w_out, zero_b, ln_mix_g[0], ln_mix_b[0], ROW_TILE)
    x1_s = _outproj_ln(x[n_prompt:n_real], oa_s, 0, ob_s, 0, w_out, zero_b, ln_mix_g[0], ln_mix_b[0],
                       n_dec)
    pre = proj[:n_prompt].reshape(bsz, t_len, -1)[:, t_len - (CONV_W - 1):]
    p_gdn_conv = pre[:, :, :GDN_CONV_CH]
    p_lru_conv = pre[:, :, c0:c0 + LRU_WIDTH]
    new = proj[n_prompt:n_real]
    s_gdn_conv = jnp.concatenate([state_gdn_conv[0][:, 1:], new[:, None, :GDN_CONV_CH]], axis=1)
    s_lru_conv = jnp.concatenate([state_lru_conv[0][:, 1:], new[:, None, c0:c0 + LRU_WIDTH]], axis=1)
    return x1_p, x1_s, p_gdn, p_gdn_conv, p_lru, p_lru_conv, s_gdn, s_gdn_conv, s_lru, s_lru_conv


def _layer1_mixers(x, bsz, t_len, n_dec, cache_swa_k, cache_swa_v, w_in_c, b_in_c, swa_sinks,
                   w_out_c, b_out_c, rel_bias, ln_mix_g, ln_mix_b):
    n_prompt = bsz * t_len
    n_real = n_prompt + n_dec
    assert t_len % WINDOW == 0 and n_dec % DEC_TILE == 0 and n_prompt % ROW_TILE == 0
    proj = _project(x, w_in_c[0].astype(BF16), b_in_c[0])
    rel = jnp.arange(WINDOW)[:, None] + WINDOW - jnp.arange(2 * WINDOW)[None, :]
    bias_vec = rel_bias.astype(F32)[_t5_bucket(jnp.arange(WINDOW))]
    bias_tab = jnp.transpose(bias_vec[jnp.clip(rel, 0, WINDOW - 1)], (2, 0, 1))
    bias_dec = bias_vec[jnp.clip(WINDOW - jnp.arange(WINDOW), 0, WINDOW - 1)].T
    bias_own = jnp.broadcast_to(bias_vec[0][:, None], (SWA_HEADS, LANES))
    sink_tab = jnp.broadcast_to(swa_sinks[0].astype(F32)[:, None], (SWA_HEADS, LANES))
    attn_p = _swa_prompt(proj, bsz, t_len, bias_tab, sink_tab)
    new = proj[n_prompt:n_real]
    kn, vn = new[:, SWA_Q_W:SWA_Q_W + SWA_KV_W], new[:, SWA_Q_W + SWA_KV_W:]
    kc = cache_swa_k[0].reshape(n_dec, WINDOW, SWA_KV_W)
    vc = cache_swa_v[0].reshape(n_dec, WINDOW, SWA_KV_W)
    attn_s = _swa_sample(new[:, :SWA_Q_W].reshape(n_dec, SWA_HEADS, SWA_HEAD_DIM), kn, vn, kc, vc,
                         bias_dec, bias_own, sink_tab).reshape(n_dec, SWA_Q_W)
    w_out = w_out_c[0].astype(BF16)
    x1_p = _outproj_ln(x, attn_p, 0, attn_p, 1, w_out, b_out_c[0], ln_mix_g[1], ln_mix_b[1], ROW_TILE)
    x1_s = _outproj_ln(x[n_prompt:n_real], attn_s, 0, attn_s, 1, w_out, b_out_c[0], ln_mix_g[1],
                       ln_mix_b[1], n_dec)
    kv_p = proj[:n_prompt].reshape(bsz, t_len, -1)[:, t_len - WINDOW:, SWA_Q_W:]
    heads = (SWA_KV_HEADS, SWA_HEAD_DIM)
    p_k = kv_p[:, :, :SWA_KV_W].reshape(bsz, WINDOW, *heads)
    p_v = kv_p[:, :, SWA_KV_W:].reshape(bsz, WINDOW, *heads)
    s_k = jnp.concatenate([kc[:, 1:], kn[:, None]], axis=1).reshape(n_dec, WINDOW, *heads)
    s_v = jnp.concatenate([vc[:, 1:], vn[:, None]], axis=1).reshape(n_dec, WINDOW, *heads)
    return x1_p, x1_s, p_k, p_v, s_k, s_v
```

```python
import functools
import math

import jax
import jax.numpy as jnp
from jax import lax
from jax.experimental import pallas as pl
from jax.experimental.pallas import tpu as pltpu

F32 = jnp.float32
BF16 = jnp.bfloat16

D_MODEL = 1024
DEPTH = 2
DN_ALPHA = (2 * DEPTH) ** 0.25
LN_EPS = 1e-5

PEER_HEADS = 8
PEER_N_KEYS = 128
PEER_D_SUB = 128
PEER_TOPK = 16
PEER_Q_W = PEER_HEADS * 2 * PEER_D_SUB

LANES = 128
SUBLANES = 8
VMEM_LIMIT = 56 * 1024 * 1024

TOK_TILE = 256
EXP_BLOCK = 2048
MXU_PARTS = 4
NEG_INF = float("-inf")


def _oddeven_merge(lo, hi, r):
    step = r * 2
    if step < hi - lo:
        yield from _oddeven_merge(lo, hi, step)
        yield from _oddeven_merge(lo + r, hi, step)
        for i in range(lo + r, hi - r, step):
            yield (i, i + r)
    else:
        yield (lo, lo + r)


def _oddeven_sort_pairs(lo, hi):
    if hi - lo >= 1:
        mid = lo + (hi - lo) // 2
        yield from _oddeven_sort_pairs(lo, mid)
        yield from _oddeven_sort_pairs(mid + 1, hi)
        yield from _oddeven_merge(lo, hi, 1)


_SORT16 = tuple(_oddeven_sort_pairs(0, 15))


def _cmpx(v, i, j):
    a, b = v[i], v[j]
    if b is None:
        return
    if a is None:
        v[i], v[j] = b, None
        return
    v[i], v[j] = jnp.maximum(a, b), jnp.minimum(a, b)


def _sort16_desc(v):
    v = list(v)
    for i, j in _SORT16:
        _cmpx(v, i, j)
    return v


def _merge_top16(x, y):
    v = []
    for k in range(16):
        a, b = x[k], y[15 - k]
        v.append(b if a is None else a if b is None else jnp.maximum(a, b))
    for d in (8, 4, 2, 1):
        for i in range(16):
            if not i & d:
                _cmpx(v, i, i + d)
    return v


def _top16_of_keys(s_t):
    v = _sort16_desc([s_t[SUBLANES * r:SUBLANES * (r + 1), :] for r in range(16)])
    for shift in (4, 2, 1):
        v = _merge_top16(v, [pltpu.roll(a, shift, 0) for a in v])
    return v


_PAIR_ROWS = [[(k, l) for l in range(PEER_TOPK // (k + 1))] for k in range(PEER_TOPK)]


def _top16_pair_sums(a, b):
    top = [a[0] + b[l] for l in range(16)]
    rest = [a[k] + b[l] for row in _PAIR_ROWS[1:] for (k, l) in row]
    for lo in range(0, len(rest), 16):
        chunk = rest[lo:lo + 16]
        chunk = chunk + [None] * (16 - len(chunk))
        top = _merge_top16(top, _sort16_desc(chunk))
    return top


def _peer_select_kernel(x_ref, wq_ref, keys_ref, xt_ref, thr_ref, a1_ref, s2_ref, b2_ref, q_scr):
    x = x_ref[...]
    xt_ref[...] = x.T.astype(BF16)
    q = jnp.dot(x.astype(BF16), wq_ref[...], preferred_element_type=F32).astype(BF16)
    for hc in range(2 * PEER_HEADS):
        q_scr[hc] = q[:, hc * PEER_D_SUB:(hc + 1) * PEER_D_SUB]

    def head(h, carry):
        for tc in range(TOK_TILE // LANES):
            rows = slice(tc * LANES, (tc + 1) * LANES)
            nt = (((1,), (1,)), ((), ()))
            s1 = lax.dot_general(keys_ref[h, 0], q_scr[2 * h, rows, :], nt,
                                 preferred_element_type=F32)
            s2 = lax.dot_general(keys_ref[h, 1], q_scr[2 * h + 1, rows, :], nt,
                                 preferred_element_type=F32)
            a = _top16_of_keys(s1)
            b = _top16_of_keys(s2)
            top = _top16_pair_sums(a, b)
            m = a[0] + b[0]
            z = jnp.exp(top[0] - m)
            for k in range(1, 16):
                z = z + jnp.exp(top[k] - m)
            theta = top[15][0:1, :]
            inv_z = (1.0 / z)[0:1, :]
            thr_ref[h, :, rows] = theta - s1
            a1_ref[h, :, rows] = jnp.exp(s1 - a[0][0:1, :]) * inv_z
            s2_ref[h, :, rows] = s2
            b2_ref[h, :, rows] = jnp.exp(s2 - b[0][0:1, :])
        return carry

    lax.fori_loop(0, PEER_HEADS, head, 0)


def _gelu_tanh(x):
    return 0.5 * x * (1.0 + jnp.tanh(math.sqrt(2.0 / math.pi) * (x + 0.044715 * (x * x * x))))


def _peer_dense_kernel(xt_ref, thr_ref, a1_ref, s2_ref, b2_ref, u_ref, vt_ref, yt_ref, act_scr, w_scr):
    e = pl.program_id(1)
    part_rows = EXP_BLOCK // MXU_PARTS

    @pl.when(e == 0)
    def _():
        yt_ref[...] = jnp.zeros_like(yt_ref)

    def gate_block(r, tc):
        lanes = slice(tc * LANES, (tc + 1) * LANES)
        rows = slice(r * PEER_N_KEYS, (r + 1) * PEER_N_KEYS)
        w = jnp.zeros((PEER_N_KEYS, LANES), F32)
        for h in range(PEER_HEADS):
            thr = thr_ref[h, r:r + 1, lanes]
            a1 = a1_ref[h, r:r + 1, lanes]
            w = w + jnp.where(s2_ref[h, :, lanes] >= thr, b2_ref[h, :, lanes], 0.0) * a1
        w_scr[rows, lanes] = (w * _gelu_tanh(act_scr[rows, lanes])).astype(BF16)

    for part in range(MXU_PARTS):
        ra = slice(part * part_rows, (part + 1) * part_rows)
        act_scr[ra, :] = jnp.dot(u_ref[ra, :], xt_ref[...], preferred_element_type=F32)
        for r in range(part * part_rows // PEER_N_KEYS, (part + 1) * part_rows // PEER_N_KEYS):
            for tc in range(TOK_TILE // LANES):
                gate_block(r, tc)
        yt_ref[...] += jnp.dot(vt_ref[:, ra], w_scr[ra, :], preferred_element_type=F32)


def _resid_ln_t_kernel(x_ref, yt_ref, g_ref, b_ref, o_ref):
    z = DN_ALPHA * x_ref[...] + yt_ref[...].T
    mu = jnp.mean(z, axis=-1, keepdims=True)
    zc = z - mu
    var = jnp.mean(zc * zc, axis=-1, keepdims=True)
    o_ref[...] = zc * lax.rsqrt(var + LN_EPS) * g_ref[...] + b_ref[...]


def _peer_layer(x, wq, keys, u, vt, ln_g, ln_b):
    n_tok = x.shape[0]
    n_tiles = n_tok // TOK_TILE
    n_exp = u.shape[0]
    fac = jax.ShapeDtypeStruct((PEER_HEADS, PEER_N_KEYS, n_tok), F32)
    fac_spec = pl.BlockSpec((PEER_HEADS, PEER_N_KEYS, TOK_TILE), lambda t: (0, 0, t))
    xt, thr, a1, s2, b2 = pl.pallas_call(
        _peer_select_kernel,
        grid=(n_tiles,),
        in_specs=[
            pl.BlockSpec((TOK_TILE, D_MODEL), lambda t: (t, 0)),
            pl.BlockSpec((D_MODEL, PEER_Q_W), lambda t: (0, 0)),
            pl.BlockSpec((PEER_HEADS, 2, PEER_N_KEYS, PEER_D_SUB), lambda t: (0, 0, 0, 0)),
        ],
        out_specs=[pl.BlockSpec((D_MODEL, TOK_TILE), lambda t: (0, t))] + [fac_spec] * 4,
        out_shape=[jax.ShapeDtypeStruct((D_MODEL, n_tok), BF16)] + [fac] * 4,
        scratch_shapes=[pltpu.VMEM((2 * PEER_HEADS, TOK_TILE, PEER_D_SUB), BF16)],
        compiler_params=pltpu.CompilerParams(dimension_semantics=("parallel",),
                                             vmem_limit_bytes=VMEM_LIMIT),
        name="peer_select",
    )(x, wq, keys)

    fac_spec2 = pl.BlockSpec((PEER_HEADS, PEER_N_KEYS, TOK_TILE), lambda t, e: (0, 0, t))
    row_spec = pl.BlockSpec((PEER_HEADS, EXP_BLOCK // PEER_N_KEYS, TOK_TILE), lambda t, e: (0, e, t))
    yt = pl.pallas_call(
        _peer_dense_kernel,
        grid=(n_tiles, n_exp // EXP_BLOCK),
        in_specs=[pl.BlockSpec((D_MODEL, TOK_TILE), lambda t, e: (0, t)),
                  row_spec, row_spec, fac_spec2, fac_spec2] + [
            pl.BlockSpec((EXP_BLOCK, D_MODEL), lambda t, e: (e, 0)),
            pl.BlockSpec((D_MODEL, EXP_BLOCK), lambda t, e: (0, e)),
        ],
        out_specs=pl.BlockSpec((D_MODEL, TOK_TILE), lambda t, e: (0, t)),
        out_shape=jax.ShapeDtypeStruct((D_MODEL, n_tok), F32),
        scratch_shapes=[pltpu.VMEM((EXP_BLOCK, TOK_TILE), F32), pltpu.VMEM((EXP_BLOCK, TOK_TILE), BF16)],
        compiler_params=pltpu.CompilerParams(dimension_semantics=("parallel", "arbitrary"),
                                             vmem_limit_bytes=VMEM_LIMIT),
        name="peer_dense",
    )(xt, thr, a1, s2, b2, u, vt)

    return pl.pallas_call(
        _resid_ln_t_kernel,
        grid=(n_tiles,),
        in_specs=[
            pl.BlockSpec((TOK_TILE, D_MODEL), lambda t: (t, 0)),
            pl.BlockSpec((D_MODEL, TOK_TILE), lambda t: (0, t)),
            pl.BlockSpec((1, D_MODEL), lambda t: (0, 0)),
            pl.BlockSpec((1, D_MODEL), lambda t: (0, 0)),
        ],
        out_specs=pl.BlockSpec((TOK_TILE, D_MODEL), lambda t: (t, 0)),
        out_shape=jax.ShapeDtypeStruct((n_tok, D_MODEL), F32),
        compiler_params=pltpu.CompilerParams(dimension_semantics=("parallel",)),
        name="peer_resid_ln",
    )(x, yt, ln_g.reshape(1, D_MODEL), ln_b.reshape(1, D_MODEL))


GDN_HEADS = 4
GDN_DK = 128
GDN_DV = 128
GDN_CHUNK = 64
CONV_W = 4
GDN_QK_W = GDN_HEADS * GDN_DK
GDN_V_W = GDN_HEADS * GDN_DV
GDN_CONV_CH = 2 * GDN_QK_W + GDN_V_W
LRU_WIDTH = 512
LRU_BLOCKS = 8
LRU_C = 8.0
SWA_HEADS = 16
SWA_KV_HEADS = 4
SWA_HEAD_DIM = 64
SWA_GROUP = SWA_HEADS // SWA_KV_HEADS
SWA_Q_W = SWA_HEADS * SWA_HEAD_DIM
SWA_KV_W = SWA_KV_HEADS * SWA_HEAD_DIM
WINDOW = 128
REL_BUCKETS = 32
REL_MAX_DIST = 128
MASKED = -1e30

SEQ_TILE = 256
ROW_TILE = 256
DEC_TILE = 8

_NT = (((1,), (1,)), ((), ()))
_TN = (((0,), (0,)), ((), ()))


def _mm(a, b):
    return jnp.dot(a.astype(BF16), b.astype(BF16), preferred_element_type=F32)


def _mm_nt(a, b):
    return lax.dot_general(a.astype(BF16), b.astype(BF16), _NT, preferred_element_type=F32)


def _mm_tn(a, b):
    return lax.dot_general(a.astype(BF16), b.astype(BF16), _TN, preferred_element_type=F32)


def _mm_split(a, b):
    a_hi = a.astype(BF16)
    b_hi = b.astype(BF16)
    a_lo = (a - a_hi.astype(F32)).astype(BF16)
    b_lo = (b - b_hi.astype(F32)).astype(BF16)
    dot = functools.partial(jnp.dot, preferred_element_type=F32)
    return dot(a_hi, b_hi) + (dot(a_hi, b_lo) + dot(a_lo, b_hi))


def _sigmoid(x):
    return 1.0 / (1.0 + jnp.exp(-x))


def _silu(x):
    return x * _sigmoid(x)


def _softplus(x):
    return jnp.maximum(x, 0.0) + jnp.log1p(jnp.exp(-jnp.abs(x)))


def _layer_norm(z, g, b):
    mu = jnp.mean(z, axis=-1, keepdims=True)
    zc = z - mu
    var = jnp.mean(zc * zc, axis=-1, keepdims=True)
    return zc * lax.rsqrt(var + LN_EPS) * g + b


def _l2_normalize(x):
    return x * lax.rsqrt(jnp.sum(x * x, axis=-1, keepdims=True) + 1e-6)


def _causal_conv(x, xp_scr, w_ref, n_rows):
    xp_scr[8:8 + n_rows, :] = x
    y = x * w_ref[CONV_W - 1:CONV_W, :]
    for i in range(CONV_W - 1):
        y = y + xp_scr[5 + i:5 + i + n_rows, :] * w_ref[i:i + 1, :]
    xp_scr[0:8, :] = x[n_rows - 8:n_rows, :]
    return y


def _gdn_gates(ab, alog_row, dtb_row):
    g = -jnp.exp(alog_row) * _softplus(ab + dtb_row)
    return g, _sigmoid(ab)


def _gdn_out_norm(o, z, nw_row):
    o = o * lax.rsqrt(jnp.mean(o * o, axis=-1, keepdims=True) + 1e-6) * nw_row
    return o * _silu(z)


def _lru_coeffs(xc, wr_ref, br_ref, wi_ref, bi_ref, lam_ref):
    r = _sigmoid(_mm(xc, wr_ref[...]) + br_ref[...])
    i = _sigmoid(_mm(xc, wi_ref[...]) + bi_ref[...])
    log_a = -LRU_C * r * _softplus(-lam_ref[...])
    a = jnp.exp(log_a)
    one_minus_a2 = -jnp.tanh(log_a) * (a * a + 1.0)
    return a, jnp.sqrt(one_minus_a2) * (i * xc)


def _proj_kernel(x_ref, w_ref, b_ref, o_ref):
    o_ref[...] = _mm(x_ref[...], w_ref[...]) + b_ref[...]


def _project(x, w, b):
    n, k = x.shape
    m = w.shape[1]
    return pl.pallas_call(
        _proj_kernel,
        grid=(n // ROW_TILE,),
        in_specs=[pl.BlockSpec((ROW_TILE, k), lambda t: (t, 0)),
                  pl.BlockSpec((k, m), lambda t: (0, 0)),
                  pl.BlockSpec((1, m), lambda t: (0, 0))],
        out_specs=pl.BlockSpec((ROW_TILE, m), lambda t: (t, 0)),
        out_shape=jax.ShapeDtypeStruct((n, m), F32),
        compiler_params=pltpu.CompilerParams(dimension_semantics=("parallel",),
                                             vmem_limit_bytes=VMEM_LIMIT),
        name="proj",
    )(x, w, b.reshape(1, m))


def _outproj_ln_kernel(x_ref, oa_ref, ob_ref, w_ref, b_ref, g_ref, beta_ref, o_ref):
    half = oa_ref.shape[1]
    y = _mm(oa_ref[...], w_ref[0:half, :]) + _mm(ob_ref[...], w_ref[half:2 * half, :]) + b_ref[...]
    o_ref[...] = _layer_norm(DN_ALPHA * x_ref[...] + y, g_ref[...], beta_ref[...])


def _outproj_ln(x, oa, oa_col, ob, ob_col, w, b, g, beta, tile):
    n = oa.shape[0]
    half = w.shape[0] // 2
    row = lambda t: (0, 0)
    return pl.pallas_call(
        _outproj_ln_kernel,
        grid=(n // tile,),
        in_specs=[pl.BlockSpec((tile, D_MODEL), lambda t: (t, 0)),
                  pl.BlockSpec((tile, half), lambda t: (t, oa_col)),
                  pl.BlockSpec((tile, half), lambda t: (t, ob_col)),
                  pl.BlockSpec((2 * half, D_MODEL), row),
                  pl.BlockSpec((1, D_MODEL), row), pl.BlockSpec((1, D_MODEL), row),
                  pl.BlockSpec((1, D_MODEL), row)],
        out_specs=pl.BlockSpec((tile, D_MODEL), lambda t: (t, 0)),
        out_shape=jax.ShapeDtypeStruct((n, D_MODEL), F32),
        compiler_params=pltpu.CompilerParams(dimension_semantics=("parallel",),
                                             vmem_limit_bytes=VMEM_LIMIT),
        name="outproj_ln",
    )(x, oa, ob, w, b.reshape(1, D_MODEL), g.reshape(1, D_MODEL), beta.reshape(1, D_MODEL))


def _gdn_prompt_kernel(qkv_ref, ab_ref, z_ref, cw_ref, alog_ref, dtb_ref, nw_ref, o_ref, s_out_ref,
                       xp_scr, s_scr, q_scr, k_scr, v_scr, gc_scr, beta_scr, u_scr, w_scr, qk_scr):
    t = pl.program_id(1)

    @pl.when(t == 0)
    def _():
        xp_scr[0:8, :] = jnp.zeros((8, GDN_CONV_CH), F32)
        s_scr[...] = jnp.zeros_like(s_scr)

    y = _silu(_causal_conv(qkv_ref[...], xp_scr, cw_ref, SEQ_TILE))
    for h in range(GDN_HEADS):
        cols = slice(h * GDN_DK, (h + 1) * GDN_DK)
        q_scr[:, cols] = _l2_normalize(y[:, cols]) * (GDN_DK ** -0.5)
        k_scr[:, cols] = _l2_normalize(y[:, GDN_QK_W + h * GDN_DK:GDN_QK_W + (h + 1) * GDN_DK])
    v_scr[...] = y[:, 2 * GDN_QK_W:]
    g, beta = _gdn_gates(ab_ref[...], alog_ref[...], dtb_ref[...])
    beta_scr[...] = beta
    pos = lax.broadcasted_iota(jnp.int32, g.shape, 0) % GDN_CHUNK
    shift = 1
    while shift < GDN_CHUNK:
        g = g + jnp.where(pos >= shift, pltpu.roll(g, shift, 0), 0.0)
        shift *= 2
    gc_scr[...] = g

    ri = lax.broadcasted_iota(jnp.int32, (GDN_CHUNK, GDN_CHUNK), 0)
    ci = lax.broadcasted_iota(jnp.int32, (GDN_CHUNK, GDN_CHUNK), 1)
    eye = ri == ci
    incl = ri >= ci
    strict = ri > ci

    n_chunks = SEQ_TILE // GDN_CHUNK
    pairs = [(c, h) for c in range(n_chunks) for h in range(GDN_HEADS)]
    rows_of = lambda c: slice(c * GDN_CHUNK, (c + 1) * GDN_CHUNK)
    cols_of = lambda h: slice(h * GDN_DK, (h + 1) * GDN_DK)
    gcol_of = lambda c, h: gc_scr[rows_of(c), h:h + 1]

    pw, inv = [], []
    for c, h in pairs:
        k = k_scr[rows_of(c), cols_of(h)]
        gcol = gcol_of(c, h)
        bcol = beta_scr[rows_of(c), GDN_HEADS + h:GDN_HEADS + h + 1]
        grow = jnp.sum(jnp.where(eye, gcol, 0.0), axis=0, keepdims=True)
        decay = jnp.where(incl, jnp.exp(jnp.where(incl, gcol - grow, 0.0)), 0.0)
        kb = k * bcol
        neg = -jnp.where(strict, _mm_nt(kb, k) * decay, 0.0)
        qk_scr[rows_of(c), h * GDN_CHUNK:(h + 1) * GDN_CHUNK] = jnp.where(
            incl, _mm_nt(q_scr[rows_of(c), cols_of(h)], k) * decay, 0.0)
        pw.append(neg)
        inv.append(jnp.where(eye, 1.0, 0.0) + neg)
    for _ in range(5):
        pw = [_mm_split(p, p) for p in pw]
        inv = [a + _mm_split(a, p) for a, p in zip(inv, pw)]
    for (c, h), a in zip(pairs, inv):
        bcol = beta_scr[rows_of(c), GDN_HEADS + h:GDN_HEADS + h + 1]
        kb = k_scr[rows_of(c), cols_of(h)] * bcol
        u_scr[rows_of(c), cols_of(h)] = _mm_split(a, v_scr[rows_of(c), cols_of(h)] * bcol)
        w_scr[rows_of(c), cols_of(h)] = _mm_split(a, kb * jnp.exp(gcol_of(c, h)))

    state = [s_scr[h] for h in range(GDN_HEADS)]
    for c in range(n_chunks):
        rows = rows_of(c)
        v_new = [u_scr[rows, cols_of(h)] - _mm(w_scr[rows, cols_of(h)], state[h])
                 for h in range(GDN_HEADS)]
        for h in range(GDN_HEADS):
            cols = cols_of(h)
            gcol = gcol_of(c, h)
            o = (_mm(q_scr[rows, cols] * jnp.exp(gcol), state[h])
                 + _mm(qk_scr[rows, h * GDN_CHUNK:(h + 1) * GDN_CHUNK], v_new[h]))
            g_last = gcol[GDN_CHUNK - 1:GDN_CHUNK, :]
            state[h] = (state[h] * jnp.exp(g_last)
                        + _mm_tn(k_scr[rows, cols] * jnp.exp(g_last - gcol), v_new[h]))
            o_ref[rows, cols] = _gdn_out_norm(o, z_ref[rows, cols], nw_ref[...])
    for h in range(GDN_HEADS):
        s_scr[h] = state[h]

    @pl.when(t == pl.num_programs(1) - 1)
    def _():
        s_out_ref[0] = s_scr[...]


def _gdn_prompt(proj, bsz, t_len, cw, alog_row, dtb_row, nw_row):
    n_t = t_len // SEQ_TILE
    row = lambda b, t: (0, 0)
    return pl.pallas_call(
        _gdn_prompt_kernel,
        grid=(bsz, n_t),
        in_specs=[pl.BlockSpec((SEQ_TILE, GDN_CONV_CH), lambda b, t: (b * n_t + t, 0)),
                  pl.BlockSpec((SEQ_TILE, LANES), lambda b, t: (b * n_t + t, 24)),
                  pl.BlockSpec((SEQ_TILE, GDN_V_W), lambda b, t: (b * n_t + t, 3)),
                  pl.BlockSpec((CONV_W, GDN_CONV_CH), row),
                  pl.BlockSpec((1, LANES), row), pl.BlockSpec((1, LANES), row),
                  pl.BlockSpec((1, GDN_DV), row)],
        out_specs=[pl.BlockSpec((SEQ_TILE, GDN_V_W), lambda b, t: (b * n_t + t, 0)),
                   pl.BlockSpec((1, GDN_HEADS, GDN_DK, GDN_DV), lambda b, t: (b, 0, 0, 0))],
        out_shape=[jax.ShapeDtypeStruct((bsz * t_len, GDN_V_W), F32),
                   jax.ShapeDtypeStruct((bsz, GDN_HEADS, GDN_DK, GDN_DV), F32)],
        scratch_shapes=[pltpu.VMEM((SEQ_TILE + 8, GDN_CONV_CH), F32),
                        pltpu.VMEM((GDN_HEADS, GDN_DK, GDN_DV), F32),
                        pltpu.VMEM((SEQ_TILE, GDN_QK_W), F32), pltpu.VMEM((SEQ_TILE, GDN_QK_W), F32),
                        pltpu.VMEM((SEQ_TILE, GDN_V_W), F32),
                        pltpu.VMEM((SEQ_TILE, LANES), F32), pltpu.VMEM((SEQ_TILE, LANES), F32),
                        pltpu.VMEM((SEQ_TILE, GDN_V_W), F32), pltpu.VMEM((SEQ_TILE, GDN_QK_W), F32),
                        pltpu.VMEM((SEQ_TILE, GDN_HEADS * GDN_CHUNK), F32)],
        compiler_params=pltpu.CompilerParams(dimension_semantics=("parallel", "arbitrary"),
                                             vmem_limit_bytes=VMEM_LIMIT),
        name="gdn_prompt",
    )(proj, proj, proj, cw, alog_row, dtb_row, nw_row)


def _lru_prompt_kernel(xr_ref, gate_ref, cw_ref, cb_ref, wr_ref, br_ref, wi_ref, bi_ref, lam_ref,
                       o_ref, h_out_ref, xp_scr, h_scr):
    t = pl.program_id(1)

    @pl.when(t == 0)
    def _():
        xp_scr[0:8, :] = jnp.zeros((8, LRU_WIDTH), F32)
        h_scr[...] = jnp.zeros_like(h_scr)

    xc = _causal_conv(xr_ref[...], xp_scr, cw_ref, SEQ_TILE) + cb_ref[...]
    a, b = _lru_coeffs(xc, wr_ref, br_ref, wi_ref, bi_ref, lam_ref)
    pos = lax.broadcasted_iota(jnp.int32, a.shape, 0)
    shift = 1
    while shift < SEQ_TILE:
        valid = pos >= shift
        b = jnp.where(valid, a * pltpu.roll(b, shift, 0) + b, b)
        a = jnp.where(valid, a * pltpu.roll(a, shift, 0), a)
        shift *= 2
    h = b + a * h_scr[...]
    h_scr[...] = h[SEQ_TILE - 1:SEQ_TILE, :]
    o_ref[...] = jax.nn.gelu(gate_ref[...]) * h

    @pl.when(t == pl.num_programs(1) - 1)
    def _():
        h_out_ref[0] = h[SEQ_TILE - 1:SEQ_TILE, :]


def _lru_prompt(proj, bsz, t_len, lru_w):
    n_t = t_len // SEQ_TILE
    row = lambda b, t: (0, 0)
    wide = pl.BlockSpec((1, LRU_WIDTH), row)
    sq = pl.BlockSpec((LRU_WIDTH, LRU_WIDTH), row)
    return pl.pallas_call(
        _lru_prompt_kernel,
        grid=(bsz, n_t),
        in_specs=[pl.BlockSpec((SEQ_TILE, LRU_WIDTH), lambda b, t: (b * n_t + t, 4)),
                  pl.BlockSpec((SEQ_TILE, LRU_WIDTH), lambda b, t: (b * n_t + t, 5)),
                  pl.BlockSpec((CONV_W, LRU_WIDTH), row), wide, sq, wide, sq, wide, wide],
        out_specs=[pl.BlockSpec((SEQ_TILE, LRU_WIDTH), lambda b, t: (b * n_t + t, 0)),
                   pl.BlockSpec((1, 1, LRU_WIDTH), lambda b, t: (b, 0, 0))],
        out_shape=[jax.ShapeDtypeStruct((bsz * t_len, LRU_WIDTH), F32),
                   jax.ShapeDtypeStruct((bsz, 1, LRU_WIDTH), F32)],
        scratch_shapes=[pltpu.VMEM((SEQ_TILE + 8, LRU_WIDTH), F32), pltpu.VMEM((1, LRU_WIDTH), F32)],
        compiler_params=pltpu.CompilerParams(dimension_semantics=("parallel", "arbitrary"),
                                             vmem_limit_bytes=VMEM_LIMIT),
        name="lru_prompt",
    )(proj, proj, *lru_w)


def _ab_sample_kernel(qkv_ref, ab_ref, z_ref, xr_ref, gate_ref, gbuf_ref, lbuf_ref, s_ref, h0_ref,
                      cw_ref, alog_ref, dtb_ref, nw_ref,
                      lcw_ref, lcb_ref, wr_ref, br_ref, wi_ref, bi_ref, lam_ref,
                      oa_ref, ob_ref, s_out_ref, h_out_ref, o_scr):
    y = qkv_ref[...] * cw_ref[CONV_W - 1:CONV_W, :]
    for i in range(CONV_W - 1):
        y = y + gbuf_ref[i] * cw_ref[i:i + 1, :]
    y = _silu(y)
    g, beta = _gdn_gates(ab_ref[...], alog_ref[...], dtb_ref[...])
    eg = jnp.exp(g)
    for h in range(GDN_HEADS):
        cols = slice(h * GDN_DK, (h + 1) * GDN_DK)
        q = _l2_normalize(y[:, cols]) * (GDN_DK ** -0.5)
        k = _l2_normalize(y[:, GDN_QK_W + h * GDN_DK:GDN_QK_W + (h + 1) * GDN_DK])
        v = y[:, 2 * GDN_QK_W + h * GDN_DV:2 * GDN_QK_W + (h + 1) * GDN_DV]
        qk = jnp.sum(q * k, axis=-1, keepdims=True)
        q_t = q.T
        k_t = k.T
        for b in range(DEC_TILE):
            s = s_ref[b, h]
            kcol = k_t[:, b:b + 1]
            e = eg[b:b + 1, h:h + 1]
            ks = jnp.sum(s * kcol, axis=0, keepdims=True)
            qs = jnp.sum(s * q_t[:, b:b + 1], axis=0, keepdims=True)
            v_new = beta[b:b + 1, GDN_HEADS + h:GDN_HEADS + h + 1] * (v[b:b + 1, :] - e * ks)
            o_scr[b:b + 1, cols] = e * qs + qk[b:b + 1, :] * v_new
            s_out_ref[b, h] = e * s + kcol * v_new
    for h in range(GDN_HEADS):
        cols = slice(h * GDN_DV, (h + 1) * GDN_DV)
        oa_ref[:, cols] = _gdn_out_norm(o_scr[:, cols], z_ref[:, cols], nw_ref[...])
    xr = xr_ref[...]
    xc = xr * lcw_ref[CONV_W - 1:CONV_W, :] + lcb_ref[...]
    for i in range(CONV_W - 1):
        xc = xc + lbuf_ref[i] * lcw_ref[i:i + 1, :]
    a, bb = _lru_coeffs(xc, wr_ref, br_ref, wi_ref, bi_ref, lam_ref)
    hid = a * h0_ref[...] + bb
    h_out_ref[...] = hid
    ob_ref[...] = jax.nn.gelu(gate_ref[...]) * hid


def _ab_sample(proj, row0, n_dec, gbuf_t, lbuf_t, s0, h0, gdn_w, lru_w):
    blk0 = row0 // DEC_TILE
    row = lambda i: (0, 0)
    wide = pl.BlockSpec((1, LRU_WIDTH), row)
    sq = pl.BlockSpec((LRU_WIDTH, LRU_WIDTH), row)
    return pl.pallas_call(
        _ab_sample_kernel,
        grid=(n_dec // DEC_TILE,),
        in_specs=[pl.BlockSpec((DEC_TILE, GDN_CONV_CH), lambda i: (blk0 + i, 0)),
                  pl.BlockSpec((DEC_TILE, LANES), lambda i: (blk0 + i, 24)),
                  pl.BlockSpec((DEC_TILE, GDN_V_W), lambda i: (blk0 + i, 3)),
                  pl.BlockSpec((DEC_TILE, LRU_WIDTH), lambda i: (blk0 + i, 4)),
                  pl.BlockSpec((DEC_TILE, LRU_WIDTH), lambda i: (blk0 + i, 5)),
                  pl.BlockSpec((CONV_W - 1, DEC_TILE, GDN_CONV_CH), lambda i: (0, i, 0)),
                  pl.BlockSpec((CONV_W - 1, DEC_TILE, LRU_WIDTH), lambda i: (0, i, 0)),
                  pl.BlockSpec((DEC_TILE, GDN_HEADS, GDN_DK, GDN_DV), lambda i: (i, 0, 0, 0)),
                  pl.BlockSpec((DEC_TILE, LRU_WIDTH), lambda i: (i, 0)),
                  pl.BlockSpec((CONV_W, GDN_CONV_CH), row),
                  pl.BlockSpec((1, LANES), row), pl.BlockSpec((1, LANES), row),
                  pl.BlockSpec((1, GDN_DV), row),
                  pl.BlockSpec((CONV_W, LRU_WIDTH), row), wide, sq, wide, sq, wide, wide],
        out_specs=[pl.BlockSpec((DEC_TILE, GDN_V_W), lambda i: (i, 0)),
                   pl.BlockSpec((DEC_TILE, LRU_WIDTH), lambda i: (i, 0)),
                   pl.BlockSpec((DEC_TILE, GDN_HEADS, GDN_DK, GDN_DV), lambda i: (i, 0, 0, 0)),
                   pl.BlockSpec((DEC_TILE, LRU_WIDTH), lambda i: (i, 0))],
        out_shape=[jax.ShapeDtypeStruct((n_dec, GDN_V_W), F32),
                   jax.ShapeDtypeStruct((n_dec, LRU_WIDTH), F32),
                   jax.ShapeDtypeStruct((n_dec, GDN_HEADS, GDN_DK, GDN_DV), F32),
                   jax.ShapeDtypeStruct((n_dec, LRU_WIDTH), F32)],
        scratch_shapes=[pltpu.VMEM((DEC_TILE, GDN_V_W), F32)],
        compiler_params=pltpu.CompilerParams(dimension_semantics=("parallel",),
                                             vmem_limit_bytes=VMEM_LIMIT),
        name="ab_sample",
    )(proj, proj, proj, proj, proj, gbuf_t, lbuf_t, s0, h0, *gdn_w, *lru_w)


def _swa_prompt_kernel(q_ref, kc_ref, kp_ref, vc_ref, vp_ref, bias_ref, sink_ref, o_ref):
    n = pl.program_id(1)
    kcat = jnp.concatenate([kp_ref[...], kc_ref[...]], axis=0)
    vcat = jnp.concatenate([vp_ref[...], vc_ref[...]], axis=0)
    qi = lax.broadcasted_iota(jnp.int32, (WINDOW, 2 * WINDOW), 0)
    ki = lax.broadcasted_iota(jnp.int32, (WINDOW, 2 * WINDOW), 1)
    rel = qi + WINDOW - ki
    mask = (rel >= 0) & (rel < WINDOW) & ((ki >= WINDOW) | (n > 0))
    head_cols = lambda h: slice(h * SWA_HEAD_DIM, (h + 1) * SWA_HEAD_DIM)
    scores = [_mm_nt(jnp.concatenate([q_ref[:, head_cols(g * SWA_GROUP + j)]
                                      for j in range(SWA_GROUP)], axis=0), kcat[:, head_cols(g)])
              for g in range(SWA_KV_HEADS)]
    probs, dens = [], []
    for g in range(SWA_KV_HEADS):
        p_rows = []
        for j in range(SWA_GROUP):
            h = g * SWA_GROUP + j
            logits = scores[g][j * WINDOW:(j + 1) * WINDOW, :] * (SWA_HEAD_DIM ** -0.5) + bias_ref[h]
            logits = jnp.where(mask, logits, MASKED)
            sink = sink_ref[h:h + 1, 0:1]
            m = jnp.maximum(jnp.max(logits, axis=-1, keepdims=True), sink)
            p = jnp.exp(logits - m)
            dens.append(jnp.sum(p, axis=-1, keepdims=True) + jnp.exp(sink - m))
            p_rows.append(p.astype(BF16))
        probs.append(jnp.concatenate(p_rows, axis=0))
    for g in range(SWA_KV_HEADS):
        acc = _mm(probs[g], vcat[:, head_cols(g)])
        for j in range(SWA_GROUP):
            h = g * SWA_GROUP + j
            o_ref[:, head_cols(h)] = acc[j * WINDOW:(j + 1) * WINDOW, :] / dens[h]


def _swa_prompt(proj, bsz, t_len, bias_tab, sink_tab):
    n_blk = t_len // WINDOW
    cur = lambda col: (lambda b, n: (b * n_blk + n, col))
    prev = lambda col: (lambda b, n: (b * n_blk + jnp.maximum(n - 1, 0), col))
    return pl.pallas_call(
        _swa_prompt_kernel,
        grid=(bsz, n_blk),
        in_specs=[pl.BlockSpec((WINDOW, SWA_Q_W), cur(0)),
                  pl.BlockSpec((WINDOW, SWA_KV_W), cur(4)), pl.BlockSpec((WINDOW, SWA_KV_W), prev(4)),
                  pl.BlockSpec((WINDOW, SWA_KV_W), cur(5)), pl.BlockSpec((WINDOW, SWA_KV_W), prev(5)),
                  pl.BlockSpec((SWA_HEADS, WINDOW, 2 * WINDOW), lambda b, n: (0, 0, 0)),
                  pl.BlockSpec((SWA_HEADS, LANES), lambda b, n: (0, 0))],
        out_specs=pl.BlockSpec((WINDOW, SWA_Q_W), lambda b, n: (b * n_blk + n, 0)),
        out_shape=jax.ShapeDtypeStruct((bsz * t_len, SWA_Q_W), F32),
        compiler_params=pltpu.CompilerParams(dimension_semantics=("parallel", "arbitrary"),
                                             vmem_limit_bytes=VMEM_LIMIT),
        name="swa_prompt",
    )(proj, proj, proj, proj, proj, bias_tab, sink_tab)


def _swa_sample_kernel(q_ref, kn_ref, vn_ref, kc_ref, vc_ref, bias_ref, bias0_ref, sink_ref, o_ref):
    lane = lax.broadcasted_iota(jnp.int32, (SWA_GROUP, WINDOW), 1)
    rnd = lambda a: a.astype(BF16).astype(F32)
    pairs = [(b, g) for b in range(DEC_TILE) for g in range(SWA_KV_HEADS)]
    heads = lambda g: slice(g * SWA_GROUP, (g + 1) * SWA_GROUP)
    kv_cols = lambda g: slice(g * SWA_HEAD_DIM, (g + 1) * SWA_HEAD_DIM)
    scores = [_mm_nt(q_ref[b, heads(g), :], kc_ref[b, :, kv_cols(g)]) for b, g in pairs]
    probs, own_p, dens = [], [], []
    for (b, g), s in zip(pairs, scores):
        logits = s * (SWA_HEAD_DIM ** -0.5) + bias_ref[heads(g), :]
        logits = jnp.where(lane >= 1, logits, MASKED)
        own = jnp.sum(rnd(q_ref[b, heads(g), :]) * rnd(kn_ref[b:b + 1, kv_cols(g)]), axis=-1,
                      keepdims=True) * (SWA_HEAD_DIM ** -0.5) + bias0_ref[heads(g), 0:1]
        sink = sink_ref[heads(g), 0:1]
        m = jnp.maximum(jnp.maximum(jnp.max(logits, axis=-1, keepdims=True), sink), own)
        p = jnp.exp(logits - m)
        pe = jnp.exp(own - m)
        probs.append(p)
        own_p.append(pe)
        dens.append(jnp.sum(p, axis=-1, keepdims=True) + pe + jnp.exp(sink - m))
    for (b, g), p, pe, den in zip(pairs, probs, own_p, dens):
        acc = _mm(p, vc_ref[b, :, kv_cols(g)]) + rnd(pe) * rnd(vn_ref[b:b + 1, kv_cols(g)])
        o_ref[b, heads(g), :] = acc / den


def _swa_sample(q3, kn, vn, kc, vc, bias_dec, bias0, sink_tab):
    n_dec = q3.shape[0]
    tab = lambda i: (0, 0)
    return pl.pallas_call(
        _swa_sample_kernel,
        grid=(n_dec // DEC_TILE,),
        in_specs=[pl.BlockSpec((DEC_TILE, SWA_HEADS, SWA_HEAD_DIM), lambda i: (i, 0, 0)),
                  pl.BlockSpec((DEC_TILE, SWA_KV_W), lambda i: (i, 0)),
                  pl.BlockSpec((DEC_TILE, SWA_KV_W), lambda i: (i, 0)),
                  pl.BlockSpec((DEC_TILE, WINDOW, SWA_KV_W), lambda i: (i, 0, 0)),
                  pl.BlockSpec((DEC_TILE, WINDOW, SWA_KV_W), lambda i: (i, 0, 0)),
                  pl.BlockSpec((SWA_HEADS, WINDOW), tab), pl.BlockSpec((SWA_HEADS, LANES), tab),
                  pl.BlockSpec((SWA_HEADS, LANES), tab)],
        out_specs=pl.BlockSpec((DEC_TILE, SWA_HEADS, SWA_HEAD_DIM), lambda i: (i, 0, 0)),
        out_shape=jax.ShapeDtypeStruct((n_dec, SWA_HEADS, SWA_HEAD_DIM), F32),
        compiler_params=pltpu.CompilerParams(dimension_semantics=("parallel",),
                                             vmem_limit_bytes=VMEM_LIMIT),
        name="swa_sample",
    )(q3, kn, vn, kc, vc, bias_dec, bias0, sink_tab)


def _t5_bucket(rel):
    exact = REL_BUCKETS // 2
    nf = jnp.maximum(rel, 1).astype(F32)
    large = exact + (jnp.log(nf / exact) / math.log(REL_MAX_DIST / exact)
                     * (REL_BUCKETS - exact)).astype(jnp.int32)
    return jnp.where(rel < exact, rel, jnp.minimum(large, REL_BUCKETS - 1))


def _lane_row(v, width=LANES):
    return jnp.zeros((1, width), F32).at[0, :v.shape[0]].set(v.astype(F32))


def kernel(x_prompt, x_sample, state_gdn, state_gdn_conv, state_lru, state_lru_conv, cache_swa_k, cache_swa_v, w_in_ab, gdn_conv_w, gdn_a_log, gdn_dt_bias, gdn_norm_w, lru_conv_w, lru_conv_b, lru_w_r, lru_b_r, lru_w_i, lru_b_i, lru_lam, w_out_ab, w_in_c, b_in_c, swa_sinks, w_out_c, b_out_c, rel_bias, ln_mix_g, ln_mix_b, ln_ffn_g, ln_ffn_b, peer_w_q, peer_keys, peer_u, peer_v):
    bsz, t_len, _ = x_prompt.shape
    n_dec = x_sample.shape[0]
    n_prompt = bsz * t_len
    assert x_sample.shape[1] == 1 and cache_swa_k.shape[2] == WINDOW

    def peer_ffn(x, layer):
        return _peer_layer(x, peer_w_q[layer].astype(BF16), peer_keys[layer].astype(BF16),
                           peer_u[layer].astype(BF16), peer_v[layer].T.astype(BF16),
                           ln_ffn_g[layer], ln_ffn_b[layer])

    x = _tokens(x_prompt.reshape(n_prompt, D_MODEL), x_sample.reshape(n_dec, D_MODEL))
    (x1_p, x1_s, p_gdn, p_gdn_conv, p_lru, p_lru_conv, s_gdn, s_gdn_conv, s_lru,
     s_lru_conv) = _layer0_mixers(
        x, bsz, t_len, n_dec, state_gdn, state_gdn_conv, state_lru, state_lru_conv, w_in_ab,
        gdn_conv_w, gdn_a_log, gdn_dt_bias, gdn_norm_w, lru_conv_w, lru_conv_b, lru_w_r, lru_b_r,
        lru_w_i, lru_b_i, lru_lam, w_out_ab, ln_mix_g, ln_mix_b)
    x = peer_ffn(_tokens(x1_p, x1_s), 0)
    x1_p, x1_s, p_k, p_v, s_k, s_v = _layer1_mixers(
        x, bsz, t_len, n_dec, cache_swa_k, cache_swa_v, w_in_c, b_in_c, swa_sinks, w_out_c, b_out_c,
        rel_bias, ln_mix_g, ln_mix_b)
    x = peer_ffn(_tokens(x1_p, x1_s), 1)

    n_real = n_prompt + n_dec
    lead = lambda a: a[None]
    return (x[:n_prompt].reshape(bsz, t_len, D_MODEL), x[n_prompt:n_real].reshape(n_dec, 1, D_MODEL),
            lead(p_gdn), lead(p_gdn_conv), lead(p_lru.reshape(bsz, LRU_WIDTH)), lead(p_lru_conv),
            lead(p_k), lead(p_v),
            lead(s_gdn), lead(s_gdn_conv), lead(s_lru), lead(s_lru_conv), lead(s_k), lead(s_v))


def _tokens(xp_rows, xs_rows):
    n_real = xp_rows.shape[0] + xs_rows.shape[0]
    n_tok = -(-n_real // TOK_TILE) * TOK_TILE
    return jnp.concatenate([xp_rows, xs_rows, jnp.zeros((n_tok - n_real, D_MODEL), F32)])


def _layer0_mixers(x, bsz, t_len, n_dec, state_gdn, state_gdn_conv, state_lru, state_lru_conv,
                   w_in_ab, gdn_conv_w, gdn_a_log, gdn_dt_bias, gdn_norm_w, lru_conv_w, lru_conv_b,
                   lru_w_r, lru_b_r, lru_w_i, lru_b_i, lru_lam, w_out_ab, ln_mix_g, ln_mix_b):
    n_prompt = bsz * t_len
    n_real = n_prompt + n_dec
    assert t_len % SEQ_TILE == 0 and n_dec % DEC_TILE == 0 and n_prompt % ROW_TILE == 0
    w_in = w_in_ab[0]
    c0 = GDN_CONV_CH + GDN_V_W
    c1 = c0 + 2 * GDN_HEADS
    w_all = jnp.concatenate([w_in[:, :c0], w_in[:, c1:], w_in[:, c0:c1],
                             jnp.zeros((D_MODEL, LANES - 2 * GDN_HEADS), F32)], axis=1).astype(BF16)
    proj = _project(x, w_all, jnp.zeros((w_all.shape[1],), F32))
    gdn_w = (gdn_conv_w[0], _lane_row(gdn_a_log[0]), _lane_row(gdn_dt_bias[0]),
             gdn_norm_w[0].reshape(1, GDN_DV))
    eye_b = jnp.eye(LRU_BLOCKS, dtype=F32)

    def block_diag(w):
        return (eye_b[:, None, :, None] * w[:, :, None, :]).reshape(LRU_WIDTH, LRU_WIDTH).astype(BF16)

    wide = lambda v: v.reshape(1, LRU_WIDTH)
    lru_w = (lru_conv_w[0], wide(lru_conv_b[0]), block_diag(lru_w_r[0]), wide(lru_b_r[0]),
             block_diag(lru_w_i[0]), wide(lru_b_i[0]), wide(lru_lam[0]))
    oa_p, p_gdn = _gdn_prompt(proj, bsz, t_len, *gdn_w)
    ob_p, p_lru = _lru_prompt(proj, bsz, t_len, lru_w)
    oa_s, ob_s, s_gdn, s_lru = _ab_sample(
        proj, n_prompt, n_dec, jnp.swapaxes(state_gdn_conv[0], 0, 1),
        jnp.swapaxes(state_lru_conv[0], 0, 1), state_gdn[0], state_lru[0], gdn_w, lru_w)
    w_out = w_out_ab[0].astype(BF16)
    zero_b = jnp.zeros((D_MODEL,), F32)
    x1_p = _outproj_ln(x, oa_p, 0, ob_p, 0, w_out, zero_b, ln_mix_g[0], ln_mix_b[0], ROW_TILE)
    x1_s = _outproj_ln(x[n_prompt:n_real], oa_s, 0, ob_s, 0, w_out, zero_b, ln_mix_g[0], ln_mix_b[0],
                       n_dec)
    pre = proj[:n_prompt].reshape(bsz, t_len, -1)[:, t_len - (CONV_W - 1):]
    p_gdn_conv = pre[:, :, :GDN_CONV_CH]
    p_lru_conv = pre[:, :, c0:c0 + LRU_WIDTH]
    new = proj[n_prompt:n_real]
    s_gdn_conv = jnp.concatenate([state_gdn_conv[0][:, 1:], new[:, None, :GDN_CONV_CH]], axis=1)
    s_lru_conv = jnp.concatenate([state_lru_conv[0][:, 1:], new[:, None, c0:c0 + LRU_WIDTH]], axis=1)
    return x1_p, x1_s, p_gdn, p_gdn_conv, p_lru, p_lru_conv, s_gdn, s_gdn_conv, s_lru, s_lru_conv


def _layer1_mixers(x, bsz, t_len, n_dec, cache_swa_k, cache_swa_v, w_in_c, b_in_c, swa_sinks,
                   w_out_c, b_out_c, rel_bias, ln_mix_g, ln_mix_b):
    n_prompt = bsz * t_len
    n_real = n_prompt + n_dec
    assert t_len % WINDOW == 0 and n_dec % DEC_TILE == 0 and n_prompt % ROW_TILE == 0
    proj = _project(x, w_in_c[0].astype(BF16), b_in_c[0])
    rel = jnp.arange(WINDOW)[:, None] + WINDOW - jnp.arange(2 * WINDOW)[None, :]
    bias_vec = rel_bias.astype(F32)[_t5_bucket(jnp.arange(WINDOW))]
    bias_tab = jnp.transpose(bias_vec[jnp.clip(rel, 0, WINDOW - 1)], (2, 0, 1))
    bias_dec = bias_vec[jnp.clip(WINDOW - jnp.arange(WINDOW), 0, WINDOW - 1)].T
    bias_own = jnp.broadcast_to(bias_vec[0][:, None], (SWA_HEADS, LANES))
    sink_tab = jnp.broadcast_to(swa_sinks[0].astype(F32)[:, None], (SWA_HEADS, LANES))
    attn_p = _swa_prompt(proj, bsz, t_len, bias_tab, sink_tab)
    new = proj[n_prompt:n_real]
    kn, vn = new[:, SWA_Q_W:SWA_Q_W + SWA_KV_W], new[:, SWA_Q_W + SWA_KV_W:]
    kc = cache_swa_k[0].reshape(n_dec, WINDOW, SWA_KV_W)
    vc = cache_swa_v[0].reshape(n_dec, WINDOW, SWA_KV_W)
    attn_s = _swa_sample(new[:, :SWA_Q_W].reshape(n_dec, SWA_HEADS, SWA_HEAD_DIM), kn, vn, kc, vc,
                         bias_dec, bias_own, sink_tab).reshape(n_dec, SWA_Q_W)
    w_out = w_out_c[0].astype(BF16)
    x1_p = _outproj_ln(x, attn_p, 0, attn_p, 1, w_out, b_out_c[0], ln_mix_g[1], ln_mix_b[1], ROW_TILE)
    x1_s = _outproj_ln(x[n_prompt:n_real], attn_s, 0, attn_s, 1, w_out, b_out_c[0], ln_mix_g[1],
                       ln_mix_b[1], n_dec)
    kv_p = proj[:n_prompt].reshape(bsz, t_len, -1)[:, t_len - WINDOW:, SWA_Q_W:]
    heads = (SWA_KV_HEADS, SWA_HEAD_DIM)
    p_k = kv_p[:, :, :SWA_KV_W].reshape(bsz, WINDOW, *heads)
    p_v = kv_p[:, :, SWA_KV_W:].reshape(bsz, WINDOW, *heads)
    s_k = jnp.concatenate([kc[:, 1:], kn[:, None]], axis=1).reshape(n_dec, WINDOW, *heads)
    s_v = jnp.concatenate([vc[:, 1:], vn[:, None]], axis=1).reshape(n_dec, WINDOW, *heads)
    return x1_p, x1_s, p_k, p_v, s_k, s_v
```

```python
import functools
import math

import jax
import jax.numpy as jnp
from jax import lax
from jax.experimental import pallas as pl
from jax.experimental.pallas import tpu as pltpu

F32 = jnp.float32
BF16 = jnp.bfloat16

D_MODEL = 1024
DEPTH = 2
DN_ALPHA = (2 * DEPTH) ** 0.25
LN_EPS = 1e-5

PEER_HEADS = 8
PEER_N_KEYS = 128
PEER_D_SUB = 128
PEER_TOPK = 16
PEER_Q_W = PEER_HEADS * 2 * PEER_D_SUB

LANES = 128
SUBLANES = 8
VMEM_LIMIT = 56 * 1024 * 1024

TOK_TILE = 256
EXP_BLOCK = 2048
MXU_PARTS = 8
NEG_INF = float("-inf")


def _oddeven_merge(lo, hi, r):
    step = r * 2
    if step < hi - lo:
        yield from _oddeven_merge(lo, hi, step)
        yield from _oddeven_merge(lo + r, hi, step)
        for i in range(lo + r, hi - r, step):
            yield (i, i + r)
    else:
        yield (lo, lo + r)


def _oddeven_sort_pairs(lo, hi):
    if hi - lo >= 1:
        mid = lo + (hi - lo) // 2
        yield from _oddeven_sort_pairs(lo, mid)
        yield from _oddeven_sort_pairs(mid + 1, hi)
        yield from _oddeven_merge(lo, hi, 1)


_SORT16 = tuple(_oddeven_sort_pairs(0, 15))


def _cmpx(v, i, j):
    a, b = v[i], v[j]
    if b is None:
        return
    if a is None:
        v[i], v[j] = b, None
        return
    v[i], v[j] = jnp.maximum(a, b), jnp.minimum(a, b)


def _sort16_desc(v):
    v = list(v)
    for i, j in _SORT16:
        _cmpx(v, i, j)
    return v


def _merge_top16(x, y):
    v = []
    for k in range(16):
        a, b = x[k], y[15 - k]
        v.append(b if a is None else a if b is None else jnp.maximum(a, b))
    for d in (8, 4, 2, 1):
        for i in range(16):
            if not i & d:
                _cmpx(v, i, i + d)
    return v


def _top16_of_keys(s_t):
    v = _sort16_desc([s_t[SUBLANES * r:SUBLANES * (r + 1), :] for r in range(16)])
    for shift in (4, 2, 1):
        v = _merge_top16(v, [pltpu.roll(a, shift, 0) for a in v])
    return v


_PAIR_ROWS = [[(k, l) for l in range(PEER_TOPK // (k + 1))] for k in range(PEER_TOPK)]


def _top16_pair_sums(a, b):
    top = [a[0] + b[l] for l in range(16)]
    rest = [a[k] + b[l] for row in _PAIR_ROWS[1:] for (k, l) in row]
    for lo in range(0, len(rest), 16):
        chunk = rest[lo:lo + 16]
        chunk = chunk + [None] * (16 - len(chunk))
        top = _merge_top16(top, _sort16_desc(chunk))
    return top


def _peer_select_kernel(x_ref, wq_ref, keys_ref, xt_ref, thr_ref, a1_ref, s2_ref, b2_ref,
                        s1_scr, top_scr, stat_scr):
    x = x_ref[...]
    xt_ref[...] = x.T.astype(BF16)
    q = jnp.dot(x.astype(BF16), wq_ref[...], preferred_element_type=F32).astype(BF16)
    nt = (((1,), (1,)), ((), ()))
    sub = lambda hc: slice(hc * PEER_D_SUB, (hc + 1) * PEER_D_SUB)
    for h in range(PEER_HEADS):
        s1_scr[h] = lax.dot_general(keys_ref[h, 0], q[:, sub(2 * h)], nt, preferred_element_type=F32)
        s2_ref[h] = lax.dot_general(keys_ref[h, 1], q[:, sub(2 * h + 1)], nt,
                                    preferred_element_type=F32)
    chunks = [slice(tc * LANES, (tc + 1) * LANES) for tc in range(TOK_TILE // LANES)]
    for h in range(PEER_HEADS):
        for lanes in chunks:
            for c, s_t in ((0, s1_scr[h, :, lanes]), (1, s2_ref[h, :, lanes])):
                for k, v in enumerate(_top16_of_keys(s_t)):
                    top_scr[c, k, h:h + 1, lanes] = v[0:1, :]
    for lanes in chunks:
        a = [top_scr[0, k, :, lanes] for k in range(PEER_TOPK)]
        b = [top_scr[1, k, :, lanes] for k in range(PEER_TOPK)]
        top = _top16_pair_sums(a, b)
        m = a[0] + b[0]
        z = jnp.exp(top[0] - m)
        for k in range(1, PEER_TOPK):
            z = z + jnp.exp(top[k] - m)
        stat_scr[0, :, lanes] = top[PEER_TOPK - 1]
        stat_scr[1, :, lanes] = 1.0 / z
    for h in range(PEER_HEADS):
        row = lambda ref, i: ref[i, h:h + 1, :]
        s1 = s1_scr[h]
        thr_ref[h] = row(stat_scr, 0) - s1
        a1_ref[h] = jnp.exp(s1 - top_scr[0, 0, h:h + 1, :]) * row(stat_scr, 1)
        b2_ref[h] = jnp.exp(s2_ref[h] - top_scr[1, 0, h:h + 1, :])


def _gelu_tanh(x):
    return 0.5 * x * (1.0 + jnp.tanh(math.sqrt(2.0 / math.pi) * (x + 0.044715 * (x * x * x))))


def _peer_dense_kernel(xt_ref, thr_ref, a1_ref, s2_ref, b2_ref, u_ref, vt_ref, yt_ref, act_scr, w_scr):
    e = pl.program_id(1)
    part_rows = EXP_BLOCK // MXU_PARTS

    @pl.when(e == 0)
    def _():
        yt_ref[...] = jnp.zeros_like(yt_ref)

    def gate_block(r, tc):
        lanes = slice(tc * LANES, (tc + 1) * LANES)
        rows = slice(r * PEER_N_KEYS, (r + 1) * PEER_N_KEYS)
        w = jnp.zeros((PEER_N_KEYS, LANES), F32)
        for h in range(PEER_HEADS):
            thr = thr_ref[h, r:r + 1, lanes]
            a1 = a1_ref[h, r:r + 1, lanes]
            w = w + jnp.where(s2_ref[h, :, lanes] >= thr, b2_ref[h, :, lanes], 0.0) * a1
        w_scr[rows, lanes] = (w * _gelu_tanh(act_scr[rows, lanes])).astype(BF16)

    for part in range(MXU_PARTS):
        ra = slice(part * part_rows, (part + 1) * part_rows)
        act_scr[ra, :] = jnp.dot(u_ref[ra, :], xt_ref[...], preferred_element_type=F32)
        for r in range(part * part_rows // PEER_N_KEYS, (part + 1) * part_rows // PEER_N_KEYS):
            for tc in range(TOK_TILE // LANES):
                gate_block(r, tc)
        yt_ref[...] += jnp.dot(vt_ref[:, ra], w_scr[ra, :], preferred_element_type=F32)


def _resid_ln_t_kernel(x_ref, yt_ref, g_ref, b_ref, o_ref):
    z = DN_ALPHA * x_ref[...] + yt_ref[...].T
    mu = jnp.mean(z, axis=-1, keepdims=True)
    zc = z - mu
    var = jnp.mean(zc * zc, axis=-1, keepdims=True)
    o_ref[...] = zc * lax.rsqrt(var + LN_EPS) * g_ref[...] + b_ref[...]


def _peer_layer(x, wq, keys, u, vt, ln_g, ln_b):
    n_tok = x.shape[0]
    n_tiles = n_tok // TOK_TILE
    n_exp = u.shape[0]
    fac = jax.ShapeDtypeStruct((PEER_HEADS, PEER_N_KEYS, n_tok), F32)
    fac_spec = pl.BlockSpec((PEER_HEADS, PEER_N_KEYS, TOK_TILE), lambda t: (0, 0, t))
    xt, thr, a1, s2, b2 = pl.pallas_call(
        _peer_select_kernel,
        grid=(n_tiles,),
        in_specs=[
            pl.BlockSpec((TOK_TILE, D_MODEL), lambda t: (t, 0)),
            pl.BlockSpec((D_MODEL, PEER_Q_W), lambda t: (0, 0)),
            pl.BlockSpec((PEER_HEADS, 2, PEER_N_KEYS, PEER_D_SUB), lambda t: (0, 0, 0, 0)),
        ],
        out_specs=[pl.BlockSpec((D_MODEL, TOK_TILE), lambda t: (0, t))] + [fac_spec] * 4,
        out_shape=[jax.ShapeDtypeStruct((D_MODEL, n_tok), BF16)] + [fac] * 4,
        scratch_shapes=[pltpu.VMEM((PEER_HEADS, PEER_N_KEYS, TOK_TILE), F32),
                        pltpu.VMEM((2, PEER_TOPK, PEER_HEADS, TOK_TILE), F32),
                        pltpu.VMEM((2, PEER_HEADS, TOK_TILE), F32)],
        compiler_params=pltpu.CompilerParams(dimension_semantics=("parallel",),
                                             vmem_limit_bytes=VMEM_LIMIT),
        name="peer_select",
    )(x, wq, keys)

    fac_spec2 = pl.BlockSpec((PEER_HEADS, PEER_N_KEYS, TOK_TILE), lambda t, e: (0, 0, t))
    row_spec = pl.BlockSpec((PEER_HEADS, EXP_BLOCK // PEER_N_KEYS, TOK_TILE), lambda t, e: (0, e, t))
    yt = pl.pallas_call(
        _peer_dense_kernel,
        grid=(n_tiles, n_exp // EXP_BLOCK),
        in_specs=[pl.BlockSpec((D_MODEL, TOK_TILE), lambda t, e: (0, t)),
                  row_spec, row_spec, fac_spec2, fac_spec2] + [
            pl.BlockSpec((EXP_BLOCK, D_MODEL), lambda t, e: (e, 0)),
            pl.BlockSpec((D_MODEL, EXP_BLOCK), lambda t, e: (0, e)),
        ],
        out_specs=pl.BlockSpec((D_MODEL, TOK_TILE), lambda t, e: (0, t)),
        out_shape=jax.ShapeDtypeStruct((D_MODEL, n_tok), F32),
        scratch_shapes=[pltpu.VMEM((EXP_BLOCK, TOK_TILE), F32), pltpu.VMEM((EXP_BLOCK, TOK_TILE), BF16)],
        compiler_params=pltpu.CompilerParams(dimension_semantics=("parallel", "arbitrary"),
                                             vmem_limit_bytes=VMEM_LIMIT),
        name="peer_dense",
    )(xt, thr, a1, s2, b2, u, vt)

    return pl.pallas_call(
        _resid_ln_t_kernel,
        grid=(n_tiles,),
        in_specs=[
            pl.BlockSpec((TOK_TILE, D_MODEL), lambda t: (t, 0)),
            pl.BlockSpec((D_MODEL, TOK_TILE), lambda t: (0, t)),
            pl.BlockSpec((1, D_MODEL), lambda t: (0, 0)),
            pl.BlockSpec((1, D_MODEL), lambda t: (0, 0)),
        ],
        out_specs=pl.BlockSpec((TOK_TILE, D_MODEL), lambda t: (t, 0)),
        out_shape=jax.ShapeDtypeStruct((n_tok, D_MODEL), F32),
        compiler_params=pltpu.CompilerParams(dimension_semantics=("parallel",)),
        name="peer_resid_ln",
    )(x, yt, ln_g.reshape(1, D_MODEL), ln_b.reshape(1, D_MODEL))


GDN_HEADS = 4
GDN_DK = 128
GDN_DV = 128
GDN_CHUNK = 64
CONV_W = 4
GDN_QK_W = GDN_HEADS * GDN_DK
GDN_V_W = GDN_HEADS * GDN_DV
GDN_CONV_CH = 2 * GDN_QK_W + GDN_V_W
LRU_WIDTH = 512
LRU_BLOCKS = 8
LRU_C = 8.0
SWA_HEADS = 16
SWA_KV_HEADS = 4
SWA_HEAD_DIM = 64
SWA_GROUP = SWA_HEADS // SWA_KV_HEADS
SWA_Q_W = SWA_HEADS * SWA_HEAD_DIM
SWA_KV_W = SWA_KV_HEADS * SWA_HEAD_DIM
WINDOW = 128
REL_BUCKETS = 32
REL_MAX_DIST = 128
MASKED = -1e30

SEQ_TILE = 256
ROW_TILE = 256
DEC_TILE = 8

_NT = (((1,), (1,)), ((), ()))
_TN = (((0,), (0,)), ((), ()))


def _mm(a, b):
    return jnp.dot(a.astype(BF16), b.astype(BF16), preferred_element_type=F32)


def _mm_nt(a, b):
    return lax.dot_general(a.astype(BF16), b.astype(BF16), _NT, preferred_element_type=F32)


def _mm_tn(a, b):
    return lax.dot_general(a.astype(BF16), b.astype(BF16), _TN, preferred_element_type=F32)


def _mm_split(a, b):
    a_hi = a.astype(BF16)
    b_hi = b.astype(BF16)
    a_lo = (a - a_hi.astype(F32)).astype(BF16)
    b_lo = (b - b_hi.astype(F32)).astype(BF16)
    dot = functools.partial(jnp.dot, preferred_element_type=F32)
    return dot(a_hi, b_hi) + (dot(a_hi, b_lo) + dot(a_lo, b_hi))


def _sigmoid(x):
    return 1.0 / (1.0 + jnp.exp(-x))


def _silu(x):
    return x * _sigmoid(x)


def _softplus(x):
    return jnp.maximum(x, 0.0) + jnp.log1p(jnp.exp(-jnp.abs(x)))


def _layer_norm(z, g, b):
    mu = jnp.mean(z, axis=-1, keepdims=True)
    zc = z - mu
    var = jnp.mean(zc * zc, axis=-1, keepdims=True)
    return zc * lax.rsqrt(var + LN_EPS) * g + b


def _l2_normalize(x):
    return x * lax.rsqrt(jnp.sum(x * x, axis=-1, keepdims=True) + 1e-6)


def _causal_conv(x, xp_scr, w_ref, n_rows):
    xp_scr[8:8 + n_rows, :] = x
    y = x * w_ref[CONV_W - 1:CONV_W, :]
    for i in range(CONV_W - 1):
        y = y + xp_scr[5 + i:5 + i + n_rows, :] * w_ref[i:i + 1, :]
    xp_scr[0:8, :] = x[n_rows - 8:n_rows, :]
    return y


def _gdn_gates(ab, alog_row, dtb_row):
    g = -jnp.exp(alog_row) * _softplus(ab + dtb_row)
    return g, _sigmoid(ab)


def _gdn_out_norm(o, z, nw_row):
    o = o * lax.rsqrt(jnp.mean(o * o, axis=-1, keepdims=True) + 1e-6) * nw_row
    return o * _silu(z)


def _lru_coeffs(xc, wr_ref, br_ref, wi_ref, bi_ref, lam_ref):
    r = _sigmoid(_mm(xc, wr_ref[...]) + br_ref[...])
    i = _sigmoid(_mm(xc, wi_ref[...]) + bi_ref[...])
    log_a = -LRU_C * r * _softplus(-lam_ref[...])
    a = jnp.exp(log_a)
    one_minus_a2 = -jnp.tanh(log_a) * (a * a + 1.0)
    return a, jnp.sqrt(one_minus_a2) * (i * xc)


def _proj_kernel(x_ref, w_ref, b_ref, o_ref):
    o_ref[...] = _mm(x_ref[...], w_ref[...]) + b_ref[...]


def _project(x, w, b):
    n, k = x.shape
    m = w.shape[1]
    return pl.pallas_call(
        _proj_kernel,
        grid=(n // ROW_TILE,),
        in_specs=[pl.BlockSpec((ROW_TILE, k), lambda t: (t, 0)),
                  pl.BlockSpec((k, m), lambda t: (0, 0)),
                  pl.BlockSpec((1, m), lambda t: (0, 0))],
        out_specs=pl.BlockSpec((ROW_TILE, m), lambda t: (t, 0)),
        out_shape=jax.ShapeDtypeStruct((n, m), F32),
        compiler_params=pltpu.CompilerParams(dimension_semantics=("parallel",),
                                             vmem_limit_bytes=VMEM_LIMIT),
        name="proj",
    )(x, w, b.reshape(1, m))


def _outproj_ln_kernel(x_ref, oa_ref, ob_ref, w_ref, b_ref, g_ref, beta_ref, o_ref):
    half = oa_ref.shape[1]
    y = _mm(oa_ref[...], w_ref[0:half, :]) + _mm(ob_ref[...], w_ref[half:2 * half, :]) + b_ref[...]
    o_ref[...] = _layer_norm(DN_ALPHA * x_ref[...] + y, g_ref[...], beta_ref[...])


def _outproj_ln(x, oa, oa_col, ob, ob_col, w, b, g, beta, tile):
    n = oa.shape[0]
    half = w.shape[0] // 2
    row = lambda t: (0, 0)
    return pl.pallas_call(
        _outproj_ln_kernel,
        grid=(n // tile,),
        in_specs=[pl.BlockSpec((tile, D_MODEL), lambda t: (t, 0)),
                  pl.BlockSpec((tile, half), lambda t: (t, oa_col)),
                  pl.BlockSpec((tile, half), lambda t: (t, ob_col)),
                  pl.BlockSpec((2 * half, D_MODEL), row),
                  pl.BlockSpec((1, D_MODEL), row), pl.BlockSpec((1, D_MODEL), row),
                  pl.BlockSpec((1, D_MODEL), row)],
        out_specs=pl.BlockSpec((tile, D_MODEL), lambda t: (t, 0)),
        out_shape=jax.ShapeDtypeStruct((n, D_MODEL), F32),
        compiler_params=pltpu.CompilerParams(dimension_semantics=("parallel",),
                                             vmem_limit_bytes=VMEM_LIMIT),
        name="outproj_ln",
    )(x, oa, ob, w, b.reshape(1, D_MODEL), g.reshape(1, D_MODEL), beta.reshape(1, D_MODEL))


def _gdn_prompt_kernel(qkv_ref, ab_ref, z_ref, cw_ref, alog_ref, dtb_ref, nw_ref, o_ref, s_out_ref,
                       xp_scr, s_scr, q_scr, k_scr, v_scr, gc_scr, beta_scr, u_scr, w_scr, qk_scr):
    t = pl.program_id(1)

    @pl.when(t == 0)
    def _():
        xp_scr[0:8, :] = jnp.zeros((8, GDN_CONV_CH), F32)
        s_scr[...] = jnp.zeros_like(s_scr)

    y = _silu(_causal_conv(qkv_ref[...], xp_scr, cw_ref, SEQ_TILE))
    for h in range(GDN_HEADS):
        cols = slice(h * GDN_DK, (h + 1) * GDN_DK)
        q_scr[:, cols] = _l2_normalize(y[:, cols]) * (GDN_DK ** -0.5)
        k_scr[:, cols] = _l2_normalize(y[:, GDN_QK_W + h * GDN_DK:GDN_QK_W + (h + 1) * GDN_DK])
    v_scr[...] = y[:, 2 * GDN_QK_W:]
    g, beta = _gdn_gates(ab_ref[...], alog_ref[...], dtb_ref[...])
    beta_scr[...] = beta
    pos = lax.broadcasted_iota(jnp.int32, g.shape, 0) % GDN_CHUNK
    shift = 1
    while shift < GDN_CHUNK:
        g = g + jnp.where(pos >= shift, pltpu.roll(g, shift, 0), 0.0)
        shift *= 2
    gc_scr[...] = g

    ri = lax.broadcasted_iota(jnp.int32, (GDN_CHUNK, GDN_CHUNK), 0)
    ci = lax.broadcasted_iota(jnp.int32, (GDN_CHUNK, GDN_CHUNK), 1)
    eye = ri == ci
    incl = ri >= ci
    strict = ri > ci

    n_chunks = SEQ_TILE // GDN_CHUNK
    pairs = [(c, h) for c in range(n_chunks) for h in range(GDN_HEADS)]
    rows_of = lambda c: slice(c * GDN_CHUNK, (c + 1) * GDN_CHUNK)
    cols_of = lambda h: slice(h * GDN_DK, (h + 1) * GDN_DK)
    gcol_of = lambda c, h: gc_scr[rows_of(c), h:h + 1]

    pw, inv = [], []
    for c, h in pairs:
        k = k_scr[rows_of(c), cols_of(h)]
        gcol = gcol_of(c, h)
        bcol = beta_scr[rows_of(c), GDN_HEADS + h:GDN_HEADS + h + 1]
        grow = jnp.sum(jnp.where(eye, gcol, 0.0), axis=0, keepdims=True)
        decay = jnp.where(incl, jnp.exp(jnp.where(incl, gcol - grow, 0.0)), 0.0)
        kb = k * bcol
        neg = -jnp.where(strict, _mm_nt(kb, k) * decay, 0.0)
        qk_scr[rows_of(c), h * GDN_CHUNK:(h + 1) * GDN_CHUNK] = jnp.where(
            incl, _mm_nt(q_scr[rows_of(c), cols_of(h)], k) * decay, 0.0)
        pw.append(neg)
        inv.append(jnp.where(eye, 1.0, 0.0) + neg)
    for _ in range(5):
        pw = [_mm_split(p, p) for p in pw]
        inv = [a + _mm_split(a, p) for a, p in zip(inv, pw)]
    for (c, h), a in zip(pairs, inv):
        bcol = beta_scr[rows_of(c), GDN_HEADS + h:GDN_HEADS + h + 1]
        kb = k_scr[rows_of(c), cols_of(h)] * bcol
        u_scr[rows_of(c), cols_of(h)] = _mm_split(a, v_scr[rows_of(c), cols_of(h)] * bcol)
        w_scr[rows_of(c), cols_of(h)] = _mm_split(a, kb * jnp.exp(gcol_of(c, h)))

    state = [s_scr[h] for h in range(GDN_HEADS)]
    for c in range(n_chunks):
        rows = rows_of(c)
        v_new = [u_scr[rows, cols_of(h)] - _mm(w_scr[rows, cols_of(h)], state[h])
                 for h in range(GDN_HEADS)]
        for h in range(GDN_HEADS):
            cols = cols_of(h)
            gcol = gcol_of(c, h)
            o = (_mm(q_scr[rows, cols] * jnp.exp(gcol), state[h])
                 + _mm(qk_scr[rows, h * GDN_CHUNK:(h + 1) * GDN_CHUNK], v_new[h]))
            g_last = gcol[GDN_CHUNK - 1:GDN_CHUNK, :]
            state[h] = (state[h] * jnp.exp(g_last)
                        + _mm_tn(k_scr[rows, cols] * jnp.exp(g_last - gcol), v_new[h]))
            o_ref[rows, cols] = _gdn_out_norm(o, z_ref[rows, cols], nw_ref[...])
    for h in range(GDN_HEADS):
        s_scr[h] = state[h]

    @pl.when(t == pl.num_programs(1) - 1)
    def _():
        s_out_ref[0] = s_scr[...]


def _gdn_prompt(proj, bsz, t_len, cw, alog_row, dtb_row, nw_row):
    n_t = t_len // SEQ_TILE
    row = lambda b, t: (0, 0)
    return pl.pallas_call(
        _gdn_prompt_kernel,
        grid=(bsz, n_t),
        in_specs=[pl.BlockSpec((SEQ_TILE, GDN_CONV_CH), lambda b, t: (b * n_t + t, 0)),
                  pl.BlockSpec((SEQ_TILE, LANES), lambda b, t: (b * n_t + t, 24)),
                  pl.BlockSpec((SEQ_TILE, GDN_V_W), lambda b, t: (b * n_t + t, 3)),
                  pl.BlockSpec((CONV_W, GDN_CONV_CH), row),
                  pl.BlockSpec((1, LANES), row), pl.BlockSpec((1, LANES), row),
                  pl.BlockSpec((1, GDN_DV), row)],
        out_specs=[pl.BlockSpec((SEQ_TILE, GDN_V_W), lambda b, t: (b * n_t + t, 0)),
                   pl.BlockSpec((1, GDN_HEADS, GDN_DK, GDN_DV), lambda b, t: (b, 0, 0, 0))],
        out_shape=[jax.ShapeDtypeStruct((bsz * t_len, GDN_V_W), F32),
                   jax.ShapeDtypeStruct((bsz, GDN_HEADS, GDN_DK, GDN_DV), F32)],
        scratch_shapes=[pltpu.VMEM((SEQ_TILE + 8, GDN_CONV_CH), F32),
                        pltpu.VMEM((GDN_HEADS, GDN_DK, GDN_DV), F32),
                        pltpu.VMEM((SEQ_TILE, GDN_QK_W), F32), pltpu.VMEM((SEQ_TILE, GDN_QK_W), F32),
                        pltpu.VMEM((SEQ_TILE, GDN_V_W), F32),
                        pltpu.VMEM((SEQ_TILE, LANES), F32), pltpu.VMEM((SEQ_TILE, LANES), F32),
                        pltpu.VMEM((SEQ_TILE, GDN_V_W), F32), pltpu.VMEM((SEQ_TILE, GDN_QK_W), F32),
                        pltpu.VMEM((SEQ_TILE, GDN_HEADS * GDN_CHUNK), F32)],
        compiler_params=pltpu.CompilerParams(dimension_semantics=("parallel", "arbitrary"),
                                             vmem_limit_bytes=VMEM_LIMIT),
        name="gdn_prompt",
    )(proj, proj, proj, cw, alog_row, dtb_row, nw_row)


def _lru_prompt_kernel(xr_ref, gate_ref, cw_ref, cb_ref, wr_ref, br_ref, wi_ref, bi_ref, lam_ref,
                       o_ref, h_out_ref, xp_scr, h_scr):
    t = pl.program_id(1)

    @pl.when(t == 0)
    def _():
        xp_scr[0:8, :] = jnp.zeros((8, LRU_WIDTH), F32)
        h_scr[...] = jnp.zeros_like(h_scr)

    xc = _causal_conv(xr_ref[...], xp_scr, cw_ref, SEQ_TILE) + cb_ref[...]
    a, b = _lru_coeffs(xc, wr_ref, br_ref, wi_ref, bi_ref, lam_ref)
    pos = lax.broadcasted_iota(jnp.int32, a.shape, 0)
    shift = 1
    while shift < SEQ_TILE:
        valid = pos >= shift
        b = jnp.where(valid, a * pltpu.roll(b, shift, 0) + b, b)
        a = jnp.where(valid, a * pltpu.roll(a, shift, 0), a)
        shift *= 2
    h = b + a * h_scr[...]
    h_scr[...] = h[SEQ_TILE - 1:SEQ_TILE, :]
    o_ref[...] = jax.nn.gelu(gate_ref[...]) * h

    @pl.when(t == pl.num_programs(1) - 1)
    def _():
        h_out_ref[0] = h[SEQ_TILE - 1:SEQ_TILE, :]


def _lru_prompt(proj, bsz, t_len, lru_w):
    n_t = t_len // SEQ_TILE
    row = lambda b, t: (0, 0)
    wide = pl.BlockSpec((1, LRU_WIDTH), row)
    sq = pl.BlockSpec((LRU_WIDTH, LRU_WIDTH), row)
    return pl.pallas_call(
        _lru_prompt_kernel,
        grid=(bsz, n_t),
        in_specs=[pl.BlockSpec((SEQ_TILE, LRU_WIDTH), lambda b, t: (b * n_t + t, 4)),
                  pl.BlockSpec((SEQ_TILE, LRU_WIDTH), lambda b, t: (b * n_t + t, 5)),
                  pl.BlockSpec((CONV_W, LRU_WIDTH), row), wide, sq, wide, sq, wide, wide],
        out_specs=[pl.BlockSpec((SEQ_TILE, LRU_WIDTH), lambda b, t: (b * n_t + t, 0)),
                   pl.BlockSpec((1, 1, LRU_WIDTH), lambda b, t: (b, 0, 0))],
        out_shape=[jax.ShapeDtypeStruct((bsz * t_len, LRU_WIDTH), F32),
                   jax.ShapeDtypeStruct((bsz, 1, LRU_WIDTH), F32)],
        scratch_shapes=[pltpu.VMEM((SEQ_TILE + 8, LRU_WIDTH), F32), pltpu.VMEM((1, LRU_WIDTH), F32)],
        compiler_params=pltpu.CompilerParams(dimension_semantics=("parallel", "arbitrary"),
                                             vmem_limit_bytes=VMEM_LIMIT),
        name="lru_prompt",
    )(proj, proj, *lru_w)


def _ab_sample_kernel(qkv_ref, ab_ref, z_ref, xr_ref, gate_ref, gbuf_ref, lbuf_ref, s_ref, h0_ref,
                      cw_ref, alog_ref, dtb_ref, nw_ref,
                      lcw_ref, lcb_ref, wr_ref, br_ref, wi_ref, bi_ref, lam_ref,
                      oa_ref, ob_ref, s_out_ref, h_out_ref, o_scr):
    y = qkv_ref[...] * cw_ref[CONV_W - 1:CONV_W, :]
    for i in range(CONV_W - 1):
        y = y + gbuf_ref[i] * cw_ref[i:i + 1, :]
    y = _silu(y)
    g, beta = _gdn_gates(ab_ref[...], alog_ref[...], dtb_ref[...])
    eg = jnp.exp(g)
    for h in range(GDN_HEADS):
        cols = slice(h * GDN_DK, (h + 1) * GDN_DK)
        q = _l2_normalize(y[:, cols]) * (GDN_DK ** -0.5)
        k = _l2_normalize(y[:, GDN_QK_W + h * GDN_DK:GDN_QK_W + (h + 1) * GDN_DK])
        v = y[:, 2 * GDN_QK_W + h * GDN_DV:2 * GDN_QK_W + (h + 1) * GDN_DV]
        qk = jnp.sum(q * k, axis=-1, keepdims=True)
        q_t = q.T
        k_t = k.T
        for b in range(DEC_TILE):
            s = s_ref[b, h]
            kcol = k_t[:, b:b + 1]
            e = eg[b:b + 1, h:h + 1]
            ks = jnp.sum(s * kcol, axis=0, keepdims=True)
            qs = jnp.sum(s * q_t[:, b:b + 1], axis=0, keepdims=True)
            v_new = beta[b:b + 1, GDN_HEADS + h:GDN_HEADS + h + 1] * (v[b:b + 1, :] - e * ks)
            o_scr[b:b + 1, cols] = e * qs + qk[b:b + 1, :] * v_new
            s_out_ref[b, h] = e * s + kcol * v_new
    for h in range(GDN_HEADS):
        cols = slice(h * GDN_DV, (h + 1) * GDN_DV)
        oa_ref[:, cols] = _gdn_out_norm(o_scr[:, cols], z_ref[:, cols], nw_ref[...])
    xr = xr_ref[...]
    xc = xr * lcw_ref[CONV_W - 1:CONV_W, :] + lcb_ref[...]
    for i in range(CONV_W - 1):
        xc = xc + lbuf_ref[i] * lcw_ref[i:i + 1, :]
    a, bb = _lru_coeffs(xc, wr_ref, br_ref, wi_ref, bi_ref, lam_ref)
    hid = a * h0_ref[...] + bb
    h_out_ref[...] = hid
    ob_ref[...] = jax.nn.gelu(gate_ref[...]) * hid


def _ab_sample(proj, row0, n_dec, gbuf_t, lbuf_t, s0, h0, gdn_w, lru_w):
    blk0 = row0 // DEC_TILE
    row = lambda i: (0, 0)
    wide = pl.BlockSpec((1, LRU_WIDTH), row)
    sq = pl.BlockSpec((LRU_WIDTH, LRU_WIDTH), row)
    return pl.pallas_call(
        _ab_sample_kernel,
        grid=(n_dec // DEC_TILE,),
        in_specs=[pl.BlockSpec((DEC_TILE, GDN_CONV_CH), lambda i: (blk0 + i, 0)),
                  pl.BlockSpec((DEC_TILE, LANES), lambda i: (blk0 + i, 24)),
                  pl.BlockSpec((DEC_TILE, GDN_V_W), lambda i: (blk0 + i, 3)),
                  pl.BlockSpec((DEC_TILE, LRU_WIDTH), lambda i: (blk0 + i, 4)),
                  pl.BlockSpec((DEC_TILE, LRU_WIDTH), lambda i: (blk0 + i, 5)),
                  pl.BlockSpec((CONV_W - 1, DEC_TILE, GDN_CONV_CH), lambda i: (0, i, 0)),
                  pl.BlockSpec((CONV_W - 1, DEC_TILE, LRU_WIDTH), lambda i: (0, i, 0)),
                  pl.BlockSpec((DEC_TILE, GDN_HEADS, GDN_DK, GDN_DV), lambda i: (i, 0, 0, 0)),
                  pl.BlockSpec((DEC_TILE, LRU_WIDTH), lambda i: (i, 0)),
                  pl.BlockSpec((CONV_W, GDN_CONV_CH), row),
                  pl.BlockSpec((1, LANES), row), pl.BlockSpec((1, LANES), row),
                  pl.BlockSpec((1, GDN_DV), row),
                  pl.BlockSpec((CONV_W, LRU_WIDTH), row), wide, sq, wide, sq, wide, wide],
        out_specs=[pl.BlockSpec((DEC_TILE, GDN_V_W), lambda i: (i, 0)),
                   pl.BlockSpec((DEC_TILE, LRU_WIDTH), lambda i: (i, 0)),
                   pl.BlockSpec((DEC_TILE, GDN_HEADS, GDN_DK, GDN_DV), lambda i: (i, 0, 0, 0)),
                   pl.BlockSpec((DEC_TILE, LRU_WIDTH), lambda i: (i, 0))],
        out_shape=[jax.ShapeDtypeStruct((n_dec, GDN_V_W), F32),
                   jax.ShapeDtypeStruct((n_dec, LRU_WIDTH), F32),
                   jax.ShapeDtypeStruct((n_dec, GDN_HEADS, GDN_DK, GDN_DV), F32),
                   jax.ShapeDtypeStruct((n_dec, LRU_WIDTH), F32)],
        scratch_shapes=[pltpu.VMEM((DEC_TILE, GDN_V_W), F32)],
        compiler_params=pltpu.CompilerParams(dimension_semantics=("parallel",),
                                             vmem_limit_bytes=VMEM_LIMIT),
        name="ab_sample",
    )(proj, proj, proj, proj, proj, gbuf_t, lbuf_t, s0, h0, *gdn_w, *lru_w)


def _swa_prompt_kernel(q_ref, kc_ref, kp_ref, vc_ref, vp_ref, bias_ref, sink_ref, o_ref):
    n = pl.program_id(1)
    kcat = jnp.concatenate([kp_ref[...], kc_ref[...]], axis=0)
    vcat = jnp.concatenate([vp_ref[...], vc_ref[...]], axis=0)
    qi = lax.broadcasted_iota(jnp.int32, (WINDOW, 2 * WINDOW), 0)
    ki = lax.broadcasted_iota(jnp.int32, (WINDOW, 2 * WINDOW), 1)
    rel = qi + WINDOW - ki
    mask = (rel >= 0) & (rel < WINDOW) & ((ki >= WINDOW) | (n > 0))
    head_cols = lambda h: slice(h * SWA_HEAD_DIM, (h + 1) * SWA_HEAD_DIM)
    scores = [_mm_nt(jnp.concatenate([q_ref[:, head_cols(g * SWA_GROUP + j)]
                                      for j in range(SWA_GROUP)], axis=0), kcat[:, head_cols(g)])
              for g in range(SWA_KV_HEADS)]
    probs, dens = [], []
    for g in range(SWA_KV_HEADS):
        p_rows = []
        for j in range(SWA_GROUP):
            h = g * SWA_GROUP + j
            logits = scores[g][j * WINDOW:(j + 1) * WINDOW, :] * (SWA_HEAD_DIM ** -0.5) + bias_ref[h]
            logits = jnp.where(mask, logits, MASKED)
            sink = sink_ref[h:h + 1, 0:1]
            m = jnp.maximum(jnp.max(logits, axis=-1, keepdims=True), sink)
            p = jnp.exp(logits - m)
            dens.append(jnp.sum(p, axis=-1, keepdims=True) + jnp.exp(sink - m))
            p_rows.append(p.astype(BF16))
        probs.append(jnp.concatenate(p_rows, axis=0))
    for g in range(SWA_KV_HEADS):
        acc = _mm(probs[g], vcat[:, head_cols(g)])
        for j in range(SWA_GROUP):
            h = g * SWA_GROUP + j
            o_ref[:, head_cols(h)] = acc[j * WINDOW:(j + 1) * WINDOW, :] / dens[h]


def _swa_prompt(proj, bsz, t_len, bias_tab, sink_tab):
    n_blk = t_len // WINDOW
    cur = lambda col: (lambda b, n: (b * n_blk + n, col))
    prev = lambda col: (lambda b, n: (b * n_blk + jnp.maximum(n - 1, 0), col))
    return pl.pallas_call(
        _swa_prompt_kernel,
        grid=(bsz, n_blk),
        in_specs=[pl.BlockSpec((WINDOW, SWA_Q_W), cur(0)),
                  pl.BlockSpec((WINDOW, SWA_KV_W), cur(4)), pl.BlockSpec((WINDOW, SWA_KV_W), prev(4)),
                  pl.BlockSpec((WINDOW, SWA_KV_W), cur(5)), pl.BlockSpec((WINDOW, SWA_KV_W), prev(5)),
                  pl.BlockSpec((SWA_HEADS, WINDOW, 2 * WINDOW), lambda b, n: (0, 0, 0)),
                  pl.BlockSpec((SWA_HEADS, LANES), lambda b, n: (0, 0))],
        out_specs=pl.BlockSpec((WINDOW, SWA_Q_W), lambda b, n: (b * n_blk + n, 0)),
        out_shape=jax.ShapeDtypeStruct((bsz * t_len, SWA_Q_W), F32),
        compiler_params=pltpu.CompilerParams(dimension_semantics=("parallel", "arbitrary"),
                                             vmem_limit_bytes=VMEM_LIMIT),
        name="swa_prompt",
    )(proj, proj, proj, proj, proj, bias_tab, sink_tab)


def _swa_sample_kernel(q_ref, kn_ref, vn_ref, kc_ref, vc_ref, bias_ref, bias0_ref, sink_ref, o_ref):
    lane = lax.broadcasted_iota(jnp.int32, (SWA_GROUP, WINDOW), 1)
    rnd = lambda a: a.astype(BF16).astype(F32)
    pairs = [(b, g) for b in range(DEC_TILE) for g in range(SWA_KV_HEADS)]
    heads = lambda g: slice(g * SWA_GROUP, (g + 1) * SWA_GROUP)
    kv_cols = lambda g: slice(g * SWA_HEAD_DIM, (g + 1) * SWA_HEAD_DIM)
    scores = [_mm_nt(q_ref[b, heads(g), :], kc_ref[b, :, kv_cols(g)]) for b, g in pairs]
    probs, own_p, dens = [], [], []
    for (b, g), s in zip(pairs, scores):
        logits = s * (SWA_HEAD_DIM ** -0.5) + bias_ref[heads(g), :]
        logits = jnp.where(lane >= 1, logits, MASKED)
        own = jnp.sum(rnd(q_ref[b, heads(g), :]) * rnd(kn_ref[b:b + 1, kv_cols(g)]), axis=-1,
                      keepdims=True) * (SWA_HEAD_DIM ** -0.5) + bias0_ref[heads(g), 0:1]
        sink = sink_ref[heads(g), 0:1]
        m = jnp.maximum(jnp.maximum(jnp.max(logits, axis=-1, keepdims=True), sink), own)
        p = jnp.exp(logits - m)
        pe = jnp.exp(own - m)
        probs.append(p)
        own_p.append(pe)
        dens.append(jnp.sum(p, axis=-1, keepdims=True) + pe + jnp.exp(sink - m))
    for (b, g), p, pe, den in zip(pairs, probs, own_p, dens):
        acc = _mm(p, vc_ref[b, :, kv_cols(g)]) + rnd(pe) * rnd(vn_ref[b:b + 1, kv_cols(g)])
        o_ref[b, heads(g), :] = acc / den


def _swa_sample(q3, kn, vn, kc, vc, bias_dec, bias0, sink_tab):
    n_dec = q3.shape[0]
    tab = lambda i: (0, 0)
    return pl.pallas_call(
        _swa_sample_kernel,
        grid=(n_dec // DEC_TILE,),
        in_specs=[pl.BlockSpec((DEC_TILE, SWA_HEADS, SWA_HEAD_DIM), lambda i: (i, 0, 0)),
                  pl.BlockSpec((DEC_TILE, SWA_KV_W), lambda i: (i, 0)),
                  pl.BlockSpec((DEC_TILE, SWA_KV_W), lambda i: (i, 0)),
                  pl.BlockSpec((DEC_TILE, WINDOW, SWA_KV_W), lambda i: (i, 0, 0)),
                  pl.BlockSpec((DEC_TILE, WINDOW, SWA_KV_W), lambda i: (i, 0, 0)),
                  pl.BlockSpec((SWA_HEADS, WINDOW), tab), pl.BlockSpec((SWA_HEADS, LANES), tab),
                  pl.BlockSpec((SWA_HEADS, LANES), tab)],
        out_specs=pl.BlockSpec((DEC_TILE, SWA_HEADS, SWA_HEAD_DIM), lambda i: (i, 0, 0)),
        out_shape=jax.ShapeDtypeStruct((n_dec, SWA_HEADS, SWA_HEAD_DIM), F32),
        compiler_params=pltpu.CompilerParams(dimension_semantics=("parallel",),
                                             vmem_limit_bytes=VMEM_LIMIT),
        name="swa_sample",
    )(q3, kn, vn, kc, vc, bias_dec, bias0, sink_tab)


def _t5_bucket(rel):
    exact = REL_BUCKETS // 2
    nf = jnp.maximum(rel, 1).astype(F32)
    large = exact + (jnp.log(nf / exact) / math.log(REL_MAX_DIST / exact)
                     * (REL_BUCKETS - exact)).astype(jnp.int32)
    return jnp.where(rel < exact, rel, jnp.minimum(large, REL_BUCKETS - 1))


def _lane_row(v, width=LANES):
    return jnp.zeros((1, width), F32).at[0, :v.shape[0]].set(v.astype(F32))


def kernel(x_prompt, x_sample, state_gdn, state_gdn_conv, state_lru, state_lru_conv, cache_swa_k, cache_swa_v, w_in_ab, gdn_conv_w, gdn_a_log, gdn_dt_bias, gdn_norm_w, lru_conv_w, lru_conv_b, lru_w_r, lru_b_r, lru_w_i, lru_b_i, lru_lam, w_out_ab, w_in_c, b_in_c, swa_sinks, w_out_c, b_out_c, rel_bias, ln_mix_g, ln_mix_b, ln_ffn_g, ln_ffn_b, peer_w_q, peer_keys, peer_u, peer_v):
    bsz, t_len, _ = x_prompt.shape
    n_dec = x_sample.shape[0]
    n_prompt = bsz * t_len
    assert x_sample.shape[1] == 1 and cache_swa_k.shape[2] == WINDOW

    def peer_ffn(x, layer):
        return _peer_layer(x, peer_w_q[layer].astype(BF16), peer_keys[layer].astype(BF16),
                           peer_u[layer].astype(BF16), peer_v[layer].T.astype(BF16),
                           ln_ffn_g[layer], ln_ffn_b[layer])

    x = _tokens(x_prompt.reshape(n_prompt, D_MODEL), x_sample.reshape(n_dec, D_MODEL))
    (x1_p, x1_s, p_gdn, p_gdn_conv, p_lru, p_lru_conv, s_gdn, s_gdn_conv, s_lru,
     s_lru_conv) = _layer0_mixers(
        x, bsz, t_len, n_dec, state_gdn, state_gdn_conv, state_lru, state_lru_conv, w_in_ab,
        gdn_conv_w, gdn_a_log, gdn_dt_bias, gdn_norm_w, lru_conv_w, lru_conv_b, lru_w_r, lru_b_r,
        lru_w_i, lru_b_i, lru_lam, w_out_ab, ln_mix_g, ln_mix_b)
    x = peer_ffn(_tokens(x1_p, x1_s), 0)
    x1_p, x1_s, p_k, p_v, s_k, s_v = _layer1_mixers(
        x, bsz, t_len, n_dec, cache_swa_k, cache_swa_v, w_in_c, b_in_c, swa_sinks, w_out_c, b_out_c,
        rel_bias, ln_mix_g, ln_mix_b)
    x = peer_ffn(_tokens(x1_p, x1_s), 1)

    n_real = n_prompt + n_dec
    lead = lambda a: a[None]
    return (x[:n_prompt].reshape(bsz, t_len, D_MODEL), x[n_prompt:n_real].reshape(n_dec, 1, D_MODEL),
            lead(p_gdn), lead(p_gdn_conv), lead(p_lru.reshape(bsz, LRU_WIDTH)), lead(p_lru_conv),
            lead(p_k), lead(p_v),
            lead(s_gdn), lead(s_gdn_conv), lead(s_lru), lead(s_lru_conv), lead(s_k), lead(s_v))


def _sequence_tails(rows, bsz, t_len, n):
    return jnp.stack([rows[(b + 1) * t_len - n:(b + 1) * t_len] for b in range(bsz)])


def _tokens(xp_rows, xs_rows):
    n_real = xp_rows.shape[0] + xs_rows.shape[0]
    n_tok = -(-n_real // TOK_TILE) * TOK_TILE
    return jnp.concatenate([xp_rows, xs_rows, jnp.zeros((n_tok - n_real, D_MODEL), F32)])


def _layer0_mixers(x, bsz, t_len, n_dec, state_gdn, state_gdn_conv, state_lru, state_lru_conv,
                   w_in_ab, gdn_conv_w, gdn_a_log, gdn_dt_bias, gdn_norm_w, lru_conv_w, lru_conv_b,
                   lru_w_r, lru_b_r, lru_w_i, lru_b_i, lru_lam, w_out_ab, ln_mix_g, ln_mix_b):
    n_prompt = bsz * t_len
    n_real = n_prompt + n_dec
    assert t_len % SEQ_TILE == 0 and n_dec % DEC_TILE == 0 and n_prompt % ROW_TILE == 0
    w_in = w_in_ab[0]
    c0 = GDN_CONV_CH + GDN_V_W
    c1 = c0 + 2 * GDN_HEADS
    w_all = jnp.concatenate([w_in[:, :c0], w_in[:, c1:], w_in[:, c0:c1],
                             jnp.zeros((D_MODEL, LANES - 2 * GDN_HEADS), F32)], axis=1).astype(BF16)
    proj = _project(x, w_all, jnp.zeros((w_all.shape[1],), F32))
    gdn_w = (gdn_conv_w[0], _lane_row(gdn_a_log[0]), _lane_row(gdn_dt_bias[0]),
             gdn_norm_w[0].reshape(1, GDN_DV))
    eye_b = jnp.eye(LRU_BLOCKS, dtype=F32)

    def block_diag(w):
        return (eye_b[:, None, :, None] * w[:, :, None, :]).reshape(LRU_WIDTH, LRU_WIDTH).astype(BF16)

    wide = lambda v: v.reshape(1, LRU_WIDTH)
    lru_w = (lru_conv_w[0], wide(lru_conv_b[0]), block_diag(lru_w_r[0]), wide(lru_b_r[0]),
             block_diag(lru_w_i[0]), wide(lru_b_i[0]), wide(lru_lam[0]))
    oa_p, p_gdn = _gdn_prompt(proj, bsz, t_len, *gdn_w)
    ob_p, p_lru = _lru_prompt(proj, bsz, t_len, lru_w)
    oa_s, ob_s, s_gdn, s_lru = _ab_sample(
        proj, n_prompt, n_dec, jnp.swapaxes(state_gdn_conv[0], 0, 1),
        jnp.swapaxes(state_lru_conv[0], 0, 1), state_gdn[0], state_lru[0], gdn_w, lru_w)
    w_out = w_out_ab[0].astype(BF16)
    zero_b = jnp.zeros((D_MODEL,), F32)
    x1_p = _outproj_ln(x, oa_p, 0, ob_p, 0, w_out, zero_b, ln_mix_g[0], ln_mix_b[0], ROW_TILE)
    x1_s = _outproj_ln(x[n_prompt:n_real], oa_s, 0, ob_s, 0, w_out, zero_b, ln_mix_g[0], ln_mix_b[0],
                       n_dec)
    pre = _sequence_tails(proj, bsz, t_len, CONV_W - 1)
    p_gdn_conv = pre[:, :, :GDN_CONV_CH]
    p_lru_conv = pre[:, :, c0:c0 + LRU_WIDTH]
    new = proj[n_prompt:n_real]
    s_gdn_conv = jnp.concatenate([state_gdn_conv[0][:, 1:], new[:, None, :GDN_CONV_CH]], axis=1)
    s_lru_conv = jnp.concatenate([state_lru_conv[0][:, 1:], new[:, None, c0:c0 + LRU_WIDTH]], axis=1)
    return x1_p, x1_s, p_gdn, p_gdn_conv, p_lru, p_lru_conv, s_gdn, s_gdn_conv, s_lru, s_lru_conv


def _layer1_mixers(x, bsz, t_len, n_dec, cache_swa_k, cache_swa_v, w_in_c, b_in_c, swa_sinks,
                   w_out_c, b_out_c, rel_bias, ln_mix_g, ln_mix_b):
    n_prompt = bsz * t_len
    n_real = n_prompt + n_dec
    assert t_len % WINDOW == 0 and n_dec % DEC_TILE == 0 and n_prompt % ROW_TILE == 0
    proj = _project(x, w_in_c[0].astype(BF16), b_in_c[0])
    rel = jnp.arange(WINDOW)[:, None] + WINDOW - jnp.arange(2 * WINDOW)[None, :]
    bias_vec = rel_bias.astype(F32)[_t5_bucket(jnp.arange(WINDOW))]
    pick = (jnp.clip(rel, 0, WINDOW - 1)[:, :, None] == jnp.arange(WINDOW)).astype(F32)
    bias_tab = jnp.einsum('qkd,dh->hqk', pick, bias_vec, precision=lax.Precision.HIGHEST)
    bias_dec = bias_vec[jnp.clip(WINDOW - jnp.arange(WINDOW), 0, WINDOW - 1)].T
    bias_own = jnp.broadcast_to(bias_vec[0][:, None], (SWA_HEADS, LANES))
    sink_tab = jnp.broadcast_to(swa_sinks[0].astype(F32)[:, None], (SWA_HEADS, LANES))
    attn_p = _swa_prompt(proj, bsz, t_len, bias_tab, sink_tab)
    new = proj[n_prompt:n_real]
    kn, vn = new[:, SWA_Q_W:SWA_Q_W + SWA_KV_W], new[:, SWA_Q_W + SWA_KV_W:]
    kc = cache_swa_k[0].reshape(n_dec, WINDOW, SWA_KV_W)
    vc = cache_swa_v[0].reshape(n_dec, WINDOW, SWA_KV_W)
    attn_s = _swa_sample(new[:, :SWA_Q_W].reshape(n_dec, SWA_HEADS, SWA_HEAD_DIM), kn, vn, kc, vc,
                         bias_dec, bias_own, sink_tab).reshape(n_dec, SWA_Q_W)
    w_out = w_out_c[0].astype(BF16)
    x1_p = _outproj_ln(x, attn_p, 0, attn_p, 1, w_out, b_out_c[0], ln_mix_g[1], ln_mix_b[1], ROW_TILE)
    x1_s = _outproj_ln(x[n_prompt:n_real], attn_s, 0, attn_s, 1, w_out, b_out_c[0], ln_mix_g[1],
                       ln_mix_b[1], n_dec)
    kv_p = _sequence_tails(proj, bsz, t_len, WINDOW)[:, :, SWA_Q_W:]
    heads = (SWA_KV_HEADS, SWA_HEAD_DIM)
    p_k = kv_p[:, :, :SWA_KV_W].reshape(bsz, WINDOW, *heads)
    p_v = kv_p[:, :, SWA_KV_W:].reshape(bsz, WINDOW, *heads)
    s_k = jnp.concatenate([kc[:, 1:], kn[:, None]], axis=1).reshape(n_dec, WINDOW, *heads)
    s_v = jnp.concatenate([vc[:, 1:], vn[:, None]], axis=1).reshape(n_dec, WINDOW, *heads)
    return x1_p, x1_s, p_k, p_v, s_k, s_v
```

```python
import functools
import math

import jax
import jax.numpy as jnp
from jax import lax
from jax.experimental import pallas as pl
from jax.experimental.pallas import tpu as pltpu

F32 = jnp.float32
BF16 = jnp.bfloat16

D_MODEL = 1024
DEPTH = 2
DN_ALPHA = (2 * DEPTH) ** 0.25
LN_EPS = 1e-5

PEER_HEADS = 8
PEER_N_KEYS = 128
PEER_D_SUB = 128
PEER_TOPK = 16
PEER_Q_W = PEER_HEADS * 2 * PEER_D_SUB

LANES = 128
SUBLANES = 8
VMEM_LIMIT = 56 * 1024 * 1024

TOK_TILE = 512
SUB_TILE = 256
EXP_BLOCK = 2048
MXU_PARTS = 8
NEG_INF = float("-inf")


def _oddeven_merge(lo, hi, r):
    step = r * 2
    if step < hi - lo:
        yield from _oddeven_merge(lo, hi, step)
        yield from _oddeven_merge(lo + r, hi, step)
        for i in range(lo + r, hi - r, step):
            yield (i, i + r)
    else:
        yield (lo, lo + r)


def _oddeven_sort_pairs(lo, hi):
    if hi - lo >= 1:
        mid = lo + (hi - lo) // 2
        yield from _oddeven_sort_pairs(lo, mid)
        yield from _oddeven_sort_pairs(mid + 1, hi)
        yield from _oddeven_merge(lo, hi, 1)


_SORT16 = tuple(_oddeven_sort_pairs(0, 15))


def _cmpx(v, i, j):
    a, b = v[i], v[j]
    if b is None:
        return
    if a is None:
        v[i], v[j] = b, None
        return
    v[i], v[j] = jnp.maximum(a, b), jnp.minimum(a, b)


def _sort16_desc(v):
    v = list(v)
    for i, j in _SORT16:
        _cmpx(v, i, j)
    return v


def _merge_top16(x, y):
    v = []
    for k in range(16):
        a, b = x[k], y[15 - k]
        v.append(b if a is None else a if b is None else jnp.maximum(a, b))
    for d in (8, 4, 2, 1):
        for i in range(16):
            if not i & d:
                _cmpx(v, i, i + d)
    return v


def _top16_of_keys(s_t):
    v = _sort16_desc([s_t[SUBLANES * r:SUBLANES * (r + 1), :] for r in range(16)])
    for shift in (4, 2, 1):
        v = _merge_top16(v, [pltpu.roll(a, shift, 0) for a in v])
    return v


_PAIR_ROWS = [[(k, l) for l in range(PEER_TOPK // (k + 1))] for k in range(PEER_TOPK)]


def _top16_pair_sums(a, b):
    top = [a[0] + b[l] for l in range(16)]
    rest = [a[k] + b[l] for row in _PAIR_ROWS[1:] for (k, l) in row]
    for lo in range(0, len(rest), 16):
        chunk = rest[lo:lo + 16]
        chunk = chunk + [None] * (16 - len(chunk))
        top = _merge_top16(top, _sort16_desc(chunk))
    return top


def _peer_select_kernel(x_ref, wq_ref, keys_ref, xt_ref, thr_ref, a1_ref, s2_ref, b2_ref,
                        s1_scr, top_scr, stat_scr):
    x = x_ref[...]
    xt_ref[...] = x.T.astype(BF16)
    q = jnp.dot(x.astype(BF16), wq_ref[...], preferred_element_type=F32).astype(BF16)
    nt = (((1,), (1,)), ((), ()))
    sub = lambda hc: slice(hc * PEER_D_SUB, (hc + 1) * PEER_D_SUB)
    chunks = [slice(tc * LANES, (tc + 1) * LANES) for tc in range(TOK_TILE // LANES)]
    for h in range(PEER_HEADS):
        s1_scr[h] = lax.dot_general(keys_ref[h, 0], q[:, sub(2 * h)], nt, preferred_element_type=F32)
        s2 = lax.dot_general(keys_ref[h, 1], q[:, sub(2 * h + 1)], nt, preferred_element_type=F32)
        for tc, lanes in enumerate(chunks):
            s2_ref[h, tc] = s2[:, lanes]
    for h in range(PEER_HEADS):
        for tc, lanes in enumerate(chunks):
            for c, s_t in ((0, s1_scr[h, :, lanes]), (1, s2_ref[h, tc])):
                for k, v in enumerate(_top16_of_keys(s_t)):
                    top_scr[c, k, h:h + 1, lanes] = v[0:1, :]
    for lanes in chunks:
        a = [top_scr[0, k, :, lanes] for k in range(PEER_TOPK)]
        b = [top_scr[1, k, :, lanes] for k in range(PEER_TOPK)]
        top = _top16_pair_sums(a, b)
        m = a[0] + b[0]
        z = jnp.exp(top[0] - m)
        for k in range(1, PEER_TOPK):
            z = z + jnp.exp(top[k] - m)
        stat_scr[0, :, lanes] = top[PEER_TOPK - 1]
        stat_scr[1, :, lanes] = 1.0 / z
    for h in range(PEER_HEADS):
        row = lambda ref, i: ref[i, h:h + 1, :]
        s1 = s1_scr[h]
        thr_ref[h] = row(stat_scr, 0) - s1
        a1_ref[h] = jnp.exp(s1 - top_scr[0, 0, h:h + 1, :]) * row(stat_scr, 1)
        for tc, lanes in enumerate(chunks):
            b2_ref[h, tc] = jnp.exp(s2_ref[h, tc] - top_scr[1, 0, h:h + 1, lanes])


def _gelu_tanh(x):
    return 0.5 * x * (1.0 + jnp.tanh(math.sqrt(2.0 / math.pi) * (x + 0.044715 * (x * x * x))))


def _peer_dense_kernel(xt_ref, thr_ref, a1_ref, s2_ref, b2_ref, u_ref, vt_ref, yt_ref, act_scr, w_scr):
    e = pl.program_id(1)
    part_rows = EXP_BLOCK // MXU_PARTS

    @pl.when(e == 0)
    def _():
        yt_ref[...] = jnp.zeros_like(yt_ref)

    def gate_block(r, tc, off):
        local = slice(tc * LANES, (tc + 1) * LANES)
        lanes = slice(off + tc * LANES, off + (tc + 1) * LANES)
        chunk = off // LANES + tc
        rows = slice(r * PEER_N_KEYS, (r + 1) * PEER_N_KEYS)
        w = jnp.zeros((PEER_N_KEYS, LANES), F32)
        for h in range(PEER_HEADS):
            thr = thr_ref[h, r:r + 1, lanes]
            a1 = a1_ref[h, r:r + 1, lanes]
            w = w + jnp.where(s2_ref[h, chunk] >= thr, b2_ref[h, chunk], 0.0) * a1
        w_scr[rows, local] = (w * _gelu_tanh(act_scr[rows, local])).astype(BF16)

    for off in range(0, TOK_TILE, SUB_TILE):
        cols = slice(off, off + SUB_TILE)
        for part in range(MXU_PARTS):
            ra = slice(part * part_rows, (part + 1) * part_rows)
            act_scr[ra, :] = jnp.dot(u_ref[ra, :], xt_ref[:, cols], preferred_element_type=F32)
            for r in range(part * part_rows // PEER_N_KEYS, (part + 1) * part_rows // PEER_N_KEYS):
                for tc in range(SUB_TILE // LANES):
                    gate_block(r, tc, off)
            yt_ref[:, cols] += jnp.dot(vt_ref[0, :, ra], w_scr[ra, :], preferred_element_type=F32)


def _resid_ln_t_kernel(x_ref, yt_ref, g_ref, b_ref, o_ref):
    z = DN_ALPHA * x_ref[...] + yt_ref[...].T
    mu = jnp.mean(z, axis=-1, keepdims=True)
    zc = z - mu
    var = jnp.mean(zc * zc, axis=-1, keepdims=True)
    o_ref[...] = zc * lax.rsqrt(var + LN_EPS) * g_ref[...] + b_ref[...]


def _peer_layer(x, wq, keys, u, vt, ln_g, ln_b):
    n_tok = x.shape[0]
    n_tiles = n_tok // TOK_TILE
    n_exp = u.shape[0]
    fac = jax.ShapeDtypeStruct((PEER_HEADS, PEER_N_KEYS, n_tok), F32)
    fac_spec = pl.BlockSpec((PEER_HEADS, PEER_N_KEYS, TOK_TILE), lambda t: (0, 0, t))
    slab = jax.ShapeDtypeStruct((PEER_HEADS, n_tok // LANES, PEER_N_KEYS, LANES), F32)
    slab_block = (PEER_HEADS, TOK_TILE // LANES, PEER_N_KEYS, LANES)
    xt, thr, a1, s2, b2 = pl.pallas_call(
        _peer_select_kernel,
        grid=(n_tiles,),
        in_specs=[
            pl.BlockSpec((TOK_TILE, D_MODEL), lambda t: (t, 0)),
            pl.BlockSpec((D_MODEL, PEER_Q_W), lambda t: (0, 0)),
            pl.BlockSpec((PEER_HEADS, 2, PEER_N_KEYS, PEER_D_SUB), lambda t: (0, 0, 0, 0)),
        ],
        out_specs=[pl.BlockSpec((D_MODEL, TOK_TILE), lambda t: (0, t)), fac_spec, fac_spec,
                   pl.BlockSpec(slab_block, lambda t: (0, t, 0, 0)),
                   pl.BlockSpec(slab_block, lambda t: (0, t, 0, 0))],
        out_shape=[jax.ShapeDtypeStruct((D_MODEL, n_tok), BF16), fac, fac, slab, slab],
        scratch_shapes=[pltpu.VMEM((PEER_HEADS, PEER_N_KEYS, TOK_TILE), F32),
                        pltpu.VMEM((2, PEER_TOPK, PEER_HEADS, TOK_TILE), F32),
                        pltpu.VMEM((2, PEER_HEADS, TOK_TILE), F32)],
        compiler_params=pltpu.CompilerParams(dimension_semantics=("parallel",),
                                             vmem_limit_bytes=VMEM_LIMIT),
        name="peer_select",
    )(x, wq, keys)

    fac_spec2 = pl.BlockSpec(slab_block, lambda t, e: (0, t, 0, 0))
    row_spec = pl.BlockSpec((PEER_HEADS, EXP_BLOCK // PEER_N_KEYS, TOK_TILE), lambda t, e: (0, e, t))
    yt = pl.pallas_call(
        _peer_dense_kernel,
        grid=(n_tiles, n_exp // EXP_BLOCK),
        in_specs=[pl.BlockSpec((D_MODEL, TOK_TILE), lambda t, e: (0, t)),
                  row_spec, row_spec, fac_spec2, fac_spec2] + [
            pl.BlockSpec((EXP_BLOCK, D_MODEL), lambda t, e: (e, 0)),
            pl.BlockSpec((1, D_MODEL, EXP_BLOCK), lambda t, e: (e, 0, 0)),
        ],
        out_specs=pl.BlockSpec((D_MODEL, TOK_TILE), lambda t, e: (0, t)),
        out_shape=jax.ShapeDtypeStruct((D_MODEL, n_tok), F32),
        scratch_shapes=[pltpu.VMEM((EXP_BLOCK, SUB_TILE), F32), pltpu.VMEM((EXP_BLOCK, SUB_TILE), BF16)],
        compiler_params=pltpu.CompilerParams(dimension_semantics=("parallel", "arbitrary"),
                                             vmem_limit_bytes=VMEM_LIMIT),
        name="peer_dense",
    )(xt, thr, a1, s2, b2, u, vt)

    return pl.pallas_call(
        _resid_ln_t_kernel,
        grid=(n_tiles,),
        in_specs=[
            pl.BlockSpec((TOK_TILE, D_MODEL), lambda t: (t, 0)),
            pl.BlockSpec((D_MODEL, TOK_TILE), lambda t: (0, t)),
            pl.BlockSpec((1, D_MODEL), lambda t: (0, 0)),
            pl.BlockSpec((1, D_MODEL), lambda t: (0, 0)),
        ],
        out_specs=pl.BlockSpec((TOK_TILE, D_MODEL), lambda t: (t, 0)),
        out_shape=jax.ShapeDtypeStruct((n_tok, D_MODEL), F32),
        compiler_params=pltpu.CompilerParams(dimension_semantics=("parallel",)),
        name="peer_resid_ln",
    )(x, yt, ln_g.reshape(1, D_MODEL), ln_b.reshape(1, D_MODEL))


GDN_HEADS = 4
GDN_DK = 128
GDN_DV = 128
GDN_CHUNK = 64
CONV_W = 4
GDN_QK_W = GDN_HEADS * GDN_DK
GDN_V_W = GDN_HEADS * GDN_DV
GDN_CONV_CH = 2 * GDN_QK_W + GDN_V_W
LRU_WIDTH = 512
LRU_BLOCKS = 8
LRU_C = 8.0
SWA_HEADS = 16
SWA_KV_HEADS = 4
SWA_HEAD_DIM = 64
SWA_GROUP = SWA_HEADS // SWA_KV_HEADS
SWA_Q_W = SWA_HEADS * SWA_HEAD_DIM
SWA_KV_W = SWA_KV_HEADS * SWA_HEAD_DIM
WINDOW = 128
REL_BUCKETS = 32
REL_MAX_DIST = 128
MASKED = -1e30

SEQ_TILE = 256
ROW_TILE = 256
DEC_TILE = 8

_NT = (((1,), (1,)), ((), ()))
_TN = (((0,), (0,)), ((), ()))


def _mm(a, b):
    return jnp.dot(a.astype(BF16), b.astype(BF16), preferred_element_type=F32)


def _mm_nt(a, b):
    return lax.dot_general(a.astype(BF16), b.astype(BF16), _NT, preferred_element_type=F32)


def _mm_tn(a, b):
    return lax.dot_general(a.astype(BF16), b.astype(BF16), _TN, preferred_element_type=F32)


def _mm_split(a, b):
    a_hi = a.astype(BF16)
    b_hi = b.astype(BF16)
    a_lo = (a - a_hi.astype(F32)).astype(BF16)
    b_lo = (b - b_hi.astype(F32)).astype(BF16)
    dot = functools.partial(jnp.dot, preferred_element_type=F32)
    return dot(a_hi, b_hi) + (dot(a_hi, b_lo) + dot(a_lo, b_hi))


def _sigmoid(x):
    return 1.0 / (1.0 + jnp.exp(-x))


def _silu(x):
    return x * _sigmoid(x)


def _softplus(x):
    return jnp.maximum(x, 0.0) + jnp.log1p(jnp.exp(-jnp.abs(x)))


def _layer_norm(z, g, b):
    mu = jnp.mean(z, axis=-1, keepdims=True)
    zc = z - mu
    var = jnp.mean(zc * zc, axis=-1, keepdims=True)
    return zc * lax.rsqrt(var + LN_EPS) * g + b


def _l2_normalize(x):
    return x * lax.rsqrt(jnp.sum(x * x, axis=-1, keepdims=True) + 1e-6)


def _causal_conv(x, xp_scr, w_ref, n_rows):
    xp_scr[8:8 + n_rows, :] = x
    y = x * w_ref[CONV_W - 1:CONV_W, :]
    for i in range(CONV_W - 1):
        y = y + xp_scr[5 + i:5 + i + n_rows, :] * w_ref[i:i + 1, :]
    xp_scr[0:8, :] = x[n_rows - 8:n_rows, :]
    return y


def _gdn_gates(ab, alog_row, dtb_row):
    g = -jnp.exp(alog_row) * _softplus(ab + dtb_row)
    return g, _sigmoid(ab)


def _gdn_out_norm(o, z, nw_row):
    o = o * lax.rsqrt(jnp.mean(o * o, axis=-1, keepdims=True) + 1e-6) * nw_row
    return o * _silu(z)


def _lru_coeffs(xc, wr_ref, br_ref, wi_ref, bi_ref, lam_ref):
    r = _sigmoid(_mm(xc, wr_ref[...]) + br_ref[...])
    i = _sigmoid(_mm(xc, wi_ref[...]) + bi_ref[...])
    log_a = -LRU_C * r * _softplus(-lam_ref[...])
    a = jnp.exp(log_a)
    one_minus_a2 = -jnp.tanh(log_a) * (a * a + 1.0)
    return a, jnp.sqrt(one_minus_a2) * (i * xc)


def _proj_kernel(x_ref, w_ref, b_ref, o_ref):
    o_ref[...] = _mm(x_ref[...], w_ref[...]) + b_ref[...]


def _project(x, w, b):
    n, k = x.shape
    m = w.shape[1]
    return pl.pallas_call(
        _proj_kernel,
        grid=(n // ROW_TILE,),
        in_specs=[pl.BlockSpec((ROW_TILE, k), lambda t: (t, 0)),
                  pl.BlockSpec((k, m), lambda t: (0, 0)),
                  pl.BlockSpec((1, m), lambda t: (0, 0))],
        out_specs=pl.BlockSpec((ROW_TILE, m), lambda t: (t, 0)),
        out_shape=jax.ShapeDtypeStruct((n, m), F32),
        compiler_params=pltpu.CompilerParams(dimension_semantics=("parallel",),
                                             vmem_limit_bytes=VMEM_LIMIT),
        name="proj",
    )(x, w, b.reshape(1, m))


def _outproj_ln_kernel(x_ref, oa_ref, ob_ref, w_ref, b_ref, g_ref, beta_ref, o_ref):
    half = oa_ref.shape[1]
    y = _mm(oa_ref[...], w_ref[0:half, :]) + _mm(ob_ref[...], w_ref[half:2 * half, :]) + b_ref[...]
    o_ref[...] = _layer_norm(DN_ALPHA * x_ref[...] + y, g_ref[...], beta_ref[...])


def _outproj_ln(x, oa, oa_col, ob, ob_col, w, b, g, beta, tile):
    n = oa.shape[0]
    half = w.shape[0] // 2
    row = lambda t: (0, 0)
    return pl.pallas_call(
        _outproj_ln_kernel,
        grid=(n // tile,),
        in_specs=[pl.BlockSpec((tile, D_MODEL), lambda t: (t, 0)),
                  pl.BlockSpec((tile, half), lambda t: (t, oa_col)),
                  pl.BlockSpec((tile, half), lambda t: (t, ob_col)),
                  pl.BlockSpec((2 * half, D_MODEL), row),
                  pl.BlockSpec((1, D_MODEL), row), pl.BlockSpec((1, D_MODEL), row),
                  pl.BlockSpec((1, D_MODEL), row)],
        out_specs=pl.BlockSpec((tile, D_MODEL), lambda t: (t, 0)),
        out_shape=jax.ShapeDtypeStruct((n, D_MODEL), F32),
        compiler_params=pltpu.CompilerParams(dimension_semantics=("parallel",),
                                             vmem_limit_bytes=VMEM_LIMIT),
        name="outproj_ln",
    )(x, oa, ob, w, b.reshape(1, D_MODEL), g.reshape(1, D_MODEL), beta.reshape(1, D_MODEL))


def _gdn_prompt_kernel(qkv_ref, ab_ref, z_ref, cw_ref, alog_ref, dtb_ref, nw_ref, o_ref, s_out_ref,
                       xp_scr, s_scr, q_scr, k_scr, v_scr, gc_scr, beta_scr, u_scr, w_scr, qk_scr):
    t = pl.program_id(1)

    @pl.when(t == 0)
    def _():
        xp_scr[0:8, :] = jnp.zeros((8, GDN_CONV_CH), F32)
        s_scr[...] = jnp.zeros_like(s_scr)

    y = _silu(_causal_conv(qkv_ref[...], xp_scr, cw_ref, SEQ_TILE))
    for h in range(GDN_HEADS):
        cols = slice(h * GDN_DK, (h + 1) * GDN_DK)
        q_scr[:, cols] = _l2_normalize(y[:, cols]) * (GDN_DK ** -0.5)
        k_scr[:, cols] = _l2_normalize(y[:, GDN_QK_W + h * GDN_DK:GDN_QK_W + (h + 1) * GDN_DK])
    v_scr[...] = y[:, 2 * GDN_QK_W:]
    g, beta = _gdn_gates(ab_ref[...], alog_ref[...], dtb_ref[...])
    beta_scr[...] = beta
    pos = lax.broadcasted_iota(jnp.int32, g.shape, 0) % GDN_CHUNK
    shift = 1
    while shift < GDN_CHUNK:
        g = g + jnp.where(pos >= shift, pltpu.roll(g, shift, 0), 0.0)
        shift *= 2
    gc_scr[...] = g

    ri = lax.broadcasted_iota(jnp.int32, (GDN_CHUNK, GDN_CHUNK), 0)
    ci = lax.broadcasted_iota(jnp.int32, (GDN_CHUNK, GDN_CHUNK), 1)
    eye = ri == ci
    incl = ri >= ci
    strict = ri > ci

    n_chunks = SEQ_TILE // GDN_CHUNK
    pairs = [(c, h) for c in range(n_chunks) for h in range(GDN_HEADS)]
    rows_of = lambda c: slice(c * GDN_CHUNK, (c + 1) * GDN_CHUNK)
    cols_of = lambda h: slice(h * GDN_DK, (h + 1) * GDN_DK)
    gcol_of = lambda c, h: gc_scr[rows_of(c), h:h + 1]

    pw, inv = [], []
    for c, h in pairs:
        k = k_scr[rows_of(c), cols_of(h)]
        gcol = gcol_of(c, h)
        bcol = beta_scr[rows_of(c), GDN_HEADS + h:GDN_HEADS + h + 1]
        grow = jnp.sum(jnp.where(eye, gcol, 0.0), axis=0, keepdims=True)
        decay = jnp.where(incl, jnp.exp(jnp.where(incl, gcol - grow, 0.0)), 0.0)
        kb = k * bcol
        neg = -jnp.where(strict, _mm_nt(kb, k) * decay, 0.0)
        qk_scr[rows_of(c), h * GDN_CHUNK:(h + 1) * GDN_CHUNK] = jnp.where(
            incl, _mm_nt(q_scr[rows_of(c), cols_of(h)], k) * decay, 0.0)
        pw.append(neg)
        inv.append(jnp.where(eye, 1.0, 0.0) + neg)
    for _ in range(5):
        pw = [_mm_split(p, p) for p in pw]
        inv = [a + _mm_split(a, p) for a, p in zip(inv, pw)]
    for (c, h), a in zip(pairs, inv):
        bcol = beta_scr[rows_of(c), GDN_HEADS + h:GDN_HEADS + h + 1]
        kb = k_scr[rows_of(c), cols_of(h)] * bcol
        u_scr[rows_of(c), cols_of(h)] = _mm_split(a, v_scr[rows_of(c), cols_of(h)] * bcol)
        w_scr[rows_of(c), cols_of(h)] = _mm_split(a, kb * jnp.exp(gcol_of(c, h)))

    state = [s_scr[h] for h in range(GDN_HEADS)]
    for c in range(n_chunks):
        rows = rows_of(c)
        v_new = [u_scr[rows, cols_of(h)] - _mm(w_scr[rows, cols_of(h)], state[h])
                 for h in range(GDN_HEADS)]
        for h in range(GDN_HEADS):
            cols = cols_of(h)
            gcol = gcol_of(c, h)
            o = (_mm(q_scr[rows, cols] * jnp.exp(gcol), state[h])
                 + _mm(qk_scr[rows, h * GDN_CHUNK:(h + 1) * GDN_CHUNK], v_new[h]))
            g_last = gcol[GDN_CHUNK - 1:GDN_CHUNK, :]
            state[h] = (state[h] * jnp.exp(g_last)
                        + _mm_tn(k_scr[rows, cols] * jnp.exp(g_last - gcol), v_new[h]))
            o_ref[rows, cols] = _gdn_out_norm(o, z_ref[rows, cols], nw_ref[...])
    for h in range(GDN_HEADS):
        s_scr[h] = state[h]

    @pl.when(t == pl.num_programs(1) - 1)
    def _():
        s_out_ref[0] = s_scr[...]


def _gdn_prompt(proj, bsz, t_len, cw, alog_row, dtb_row, nw_row):
    n_t = t_len // SEQ_TILE
    row = lambda b, t: (0, 0)
    return pl.pallas_call(
        _gdn_prompt_kernel,
        grid=(bsz, n_t),
        in_specs=[pl.BlockSpec((SEQ_TILE, GDN_CONV_CH), lambda b, t: (b * n_t + t, 0)),
                  pl.BlockSpec((SEQ_TILE, LANES), lambda b, t: (b * n_t + t, 24)),
                  pl.BlockSpec((SEQ_TILE, GDN_V_W), lambda b, t: (b * n_t + t, 3)),
                  pl.BlockSpec((CONV_W, GDN_CONV_CH), row),
                  pl.BlockSpec((1, LANES), row), pl.BlockSpec((1, LANES), row),
                  pl.BlockSpec((1, GDN_DV), row)],
        out_specs=[pl.BlockSpec((SEQ_TILE, GDN_V_W), lambda b, t: (b * n_t + t, 0)),
                   pl.BlockSpec((1, GDN_HEADS, GDN_DK, GDN_DV), lambda b, t: (b, 0, 0, 0))],
        out_shape=[jax.ShapeDtypeStruct((bsz * t_len, GDN_V_W), F32),
                   jax.ShapeDtypeStruct((bsz, GDN_HEADS, GDN_DK, GDN_DV), F32)],
        scratch_shapes=[pltpu.VMEM((SEQ_TILE + 8, GDN_CONV_CH), F32),
                        pltpu.VMEM((GDN_HEADS, GDN_DK, GDN_DV), F32),
                        pltpu.VMEM((SEQ_TILE, GDN_QK_W), F32), pltpu.VMEM((SEQ_TILE, GDN_QK_W), F32),
                        pltpu.VMEM((SEQ_TILE, GDN_V_W), F32),
                        pltpu.VMEM((SEQ_TILE, LANES), F32), pltpu.VMEM((SEQ_TILE, LANES), F32),
                        pltpu.VMEM((SEQ_TILE, GDN_V_W), F32), pltpu.VMEM((SEQ_TILE, GDN_QK_W), F32),
                        pltpu.VMEM((SEQ_TILE, GDN_HEADS * GDN_CHUNK), F32)],
        compiler_params=pltpu.CompilerParams(dimension_semantics=("parallel", "arbitrary"),
                                             vmem_limit_bytes=VMEM_LIMIT),
        name="gdn_prompt",
    )(proj, proj, proj, cw, alog_row, dtb_row, nw_row)


def _lru_prompt_kernel(xr_ref, gate_ref, cw_ref, cb_ref, wr_ref, br_ref, wi_ref, bi_ref, lam_ref,
                       o_ref, h_out_ref, xp_scr, h_scr):
    t = pl.program_id(1)

    @pl.when(t == 0)
    def _():
        xp_scr[0:8, :] = jnp.zeros((8, LRU_WIDTH), F32)
        h_scr[...] = jnp.zeros_like(h_scr)

    xc = _causal_conv(xr_ref[...], xp_scr, cw_ref, SEQ_TILE) + cb_ref[...]
    a, b = _lru_coeffs(xc, wr_ref, br_ref, wi_ref, bi_ref, lam_ref)
    pos = lax.broadcasted_iota(jnp.int32, a.shape, 0)
    shift = 1
    while shift < SEQ_TILE:
        valid = pos >= shift
        b = jnp.where(valid, a * pltpu.roll(b, shift, 0) + b, b)
        a = jnp.where(valid, a * pltpu.roll(a, shift, 0), a)
        shift *= 2
    h = b + a * h_scr[...]
    h_scr[...] = h[SEQ_TILE - 1:SEQ_TILE, :]
    o_ref[...] = jax.nn.gelu(gate_ref[...]) * h

    @pl.when(t == pl.num_programs(1) - 1)
    def _():
        h_out_ref[0] = h[SEQ_TILE - 1:SEQ_TILE, :]


def _lru_prompt(proj, bsz, t_len, lru_w):
    n_t = t_len // SEQ_TILE
    row = lambda b, t: (0, 0)
    wide = pl.BlockSpec((1, LRU_WIDTH), row)
    sq = pl.BlockSpec((LRU_WIDTH, LRU_WIDTH), row)
    return pl.pallas_call(
        _lru_prompt_kernel,
        grid=(bsz, n_t),
        in_specs=[pl.BlockSpec((SEQ_TILE, LRU_WIDTH), lambda b, t: (b * n_t + t, 4)),
                  pl.BlockSpec((SEQ_TILE, LRU_WIDTH), lambda b, t: (b * n_t + t, 5)),
                  pl.BlockSpec((CONV_W, LRU_WIDTH), row), wide, sq, wide, sq, wide, wide],
        out_specs=[pl.BlockSpec((SEQ_TILE, LRU_WIDTH), lambda b, t: (b * n_t + t, 0)),
                   pl.BlockSpec((1, 1, LRU_WIDTH), lambda b, t: (b, 0, 0))],
        out_shape=[jax.ShapeDtypeStruct((bsz * t_len, LRU_WIDTH), F32),
                   jax.ShapeDtypeStruct((bsz, 1, LRU_WIDTH), F32)],
        scratch_shapes=[pltpu.VMEM((SEQ_TILE + 8, LRU_WIDTH), F32), pltpu.VMEM((1, LRU_WIDTH), F32)],
        compiler_params=pltpu.CompilerParams(dimension_semantics=("parallel", "arbitrary"),
                                             vmem_limit_bytes=VMEM_LIMIT),
        name="lru_prompt",
    )(proj, proj, *lru_w)


def _ab_sample_kernel(qkv_ref, ab_ref, z_ref, xr_ref, gate_ref, gbuf_ref, lbuf_ref, s_ref, h0_ref,
                      cw_ref, alog_ref, dtb_ref, nw_ref,
                      lcw_ref, lcb_ref, wr_ref, br_ref, wi_ref, bi_ref, lam_ref,
                      oa_ref, ob_ref, s_out_ref, h_out_ref, o_scr):
    y = qkv_ref[...] * cw_ref[CONV_W - 1:CONV_W, :]
    for i in range(CONV_W - 1):
        y = y + gbuf_ref[i] * cw_ref[i:i + 1, :]
    y = _silu(y)
    g, beta = _gdn_gates(ab_ref[...], alog_ref[...], dtb_ref[...])
    eg = jnp.exp(g)
    for h in range(GDN_HEADS):
        cols = slice(h * GDN_DK, (h + 1) * GDN_DK)
        q = _l2_normalize(y[:, cols]) * (GDN_DK ** -0.5)
        k = _l2_normalize(y[:, GDN_QK_W + h * GDN_DK:GDN_QK_W + (h + 1) * GDN_DK])
        v = y[:, 2 * GDN_QK_W + h * GDN_DV:2 * GDN_QK_W + (h + 1) * GDN_DV]
        qk = jnp.sum(q * k, axis=-1, keepdims=True)
        q_t = q.T
        k_t = k.T
        for b in range(DEC_TILE):
            s = s_ref[b, h]
            kcol = k_t[:, b:b + 1]
            e = eg[b:b + 1, h:h + 1]
            ks = jnp.sum(s * kcol, axis=0, keepdims=True)
            qs = jnp.sum(s * q_t[:, b:b + 1], axis=0, keepdims=True)
            v_new = beta[b:b + 1, GDN_HEADS + h:GDN_HEADS + h + 1] * (v[b:b + 1, :] - e * ks)
            o_scr[b:b + 1, cols] = e * qs + qk[b:b + 1, :] * v_new
            s_out_ref[b, h] = e * s + kcol * v_new
    for h in range(GDN_HEADS):
        cols = slice(h * GDN_DV, (h + 1) * GDN_DV)
        oa_ref[:, cols] = _gdn_out_norm(o_scr[:, cols], z_ref[:, cols], nw_ref[...])
    xr = xr_ref[...]
    xc = xr * lcw_ref[CONV_W - 1:CONV_W, :] + lcb_ref[...]
    for i in range(CONV_W - 1):
        xc = xc + lbuf_ref[i] * lcw_ref[i:i + 1, :]
    a, bb = _lru_coeffs(xc, wr_ref, br_ref, wi_ref, bi_ref, lam_ref)
    hid = a * h0_ref[...] + bb
    h_out_ref[...] = hid
    ob_ref[...] = jax.nn.gelu(gate_ref[...]) * hid


def _ab_sample(proj, row0, n_dec, gbuf_t, lbuf_t, s0, h0, gdn_w, lru_w):
    blk0 = row0 // DEC_TILE
    row = lambda i: (0, 0)
    wide = pl.BlockSpec((1, LRU_WIDTH), row)
    sq = pl.BlockSpec((LRU_WIDTH, LRU_WIDTH), row)
    return pl.pallas_call(
        _ab_sample_kernel,
        grid=(n_dec // DEC_TILE,),
        in_specs=[pl.BlockSpec((DEC_TILE, GDN_CONV_CH), lambda i: (blk0 + i, 0)),
                  pl.BlockSpec((DEC_TILE, LANES), lambda i: (blk0 + i, 24)),
                  pl.BlockSpec((DEC_TILE, GDN_V_W), lambda i: (blk0 + i, 3)),
                  pl.BlockSpec((DEC_TILE, LRU_WIDTH), lambda i: (blk0 + i, 4)),
                  pl.BlockSpec((DEC_TILE, LRU_WIDTH), lambda i: (blk0 + i, 5)),
                  pl.BlockSpec((CONV_W - 1, DEC_TILE, GDN_CONV_CH), lambda i: (0, i, 0)),
                  pl.BlockSpec((CONV_W - 1, DEC_TILE, LRU_WIDTH), lambda i: (0, i, 0)),
                  pl.BlockSpec((DEC_TILE, GDN_HEADS, GDN_DK, GDN_DV), lambda i: (i, 0, 0, 0)),
                  pl.BlockSpec((DEC_TILE, LRU_WIDTH), lambda i: (i, 0)),
                  pl.BlockSpec((CONV_W, GDN_CONV_CH), row),
                  pl.BlockSpec((1, LANES), row), pl.BlockSpec((1, LANES), row),
                  pl.BlockSpec((1, GDN_DV), row),
                  pl.BlockSpec((CONV_W, LRU_WIDTH), row), wide, sq, wide, sq, wide, wide],
        out_specs=[pl.BlockSpec((DEC_TILE, GDN_V_W), lambda i: (i, 0)),
                   pl.BlockSpec((DEC_TILE, LRU_WIDTH), lambda i: (i, 0)),
                   pl.BlockSpec((DEC_TILE, GDN_HEADS, GDN_DK, GDN_DV), lambda i: (i, 0, 0, 0)),
                   pl.BlockSpec((DEC_TILE, LRU_WIDTH), lambda i: (i, 0))],
        out_shape=[jax.ShapeDtypeStruct((n_dec, GDN_V_W), F32),
                   jax.ShapeDtypeStruct((n_dec, LRU_WIDTH), F32),
                   jax.ShapeDtypeStruct((n_dec, GDN_HEADS, GDN_DK, GDN_DV), F32),
                   jax.ShapeDtypeStruct((n_dec, LRU_WIDTH), F32)],
        scratch_shapes=[pltpu.VMEM((DEC_TILE, GDN_V_W), F32)],
        compiler_params=pltpu.CompilerParams(dimension_semantics=("parallel",),
                                             vmem_limit_bytes=VMEM_LIMIT),
        name="ab_sample",
    )(proj, proj, proj, proj, proj, gbuf_t, lbuf_t, s0, h0, *gdn_w, *lru_w)


def _swa_prompt_kernel(q_ref, kc_ref, kp_ref, vc_ref, vp_ref, bias_ref, sink_ref, o_ref):
    n = pl.program_id(1)
    kcat = jnp.concatenate([kp_ref[...], kc_ref[...]], axis=0)
    vcat = jnp.concatenate([vp_ref[...], vc_ref[...]], axis=0)
    qi = lax.broadcasted_iota(jnp.int32, (WINDOW, 2 * WINDOW), 0)
    ki = lax.broadcasted_iota(jnp.int32, (WINDOW, 2 * WINDOW), 1)
    rel = qi + WINDOW - ki
    mask = (rel >= 0) & (rel < WINDOW) & ((ki >= WINDOW) | (n > 0))
    head_cols = lambda h: slice(h * SWA_HEAD_DIM, (h + 1) * SWA_HEAD_DIM)
    scores = [_mm_nt(jnp.concatenate([q_ref[:, head_cols(g * SWA_GROUP + j)]
                                      for j in range(SWA_GROUP)], axis=0), kcat[:, head_cols(g)])
              for g in range(SWA_KV_HEADS)]
    probs, dens = [], []
    for g in range(SWA_KV_HEADS):
        p_rows = []
        for j in range(SWA_GROUP):
            h = g * SWA_GROUP + j
            logits = scores[g][j * WINDOW:(j + 1) * WINDOW, :] * (SWA_HEAD_DIM ** -0.5) + bias_ref[h]
            logits = jnp.where(mask, logits, MASKED)
            sink = sink_ref[h:h + 1, 0:1]
            m = jnp.maximum(jnp.max(logits, axis=-1, keepdims=True), sink)
            p = jnp.exp(logits - m)
            dens.append(jnp.sum(p, axis=-1, keepdims=True) + jnp.exp(sink - m))
            p_rows.append(p.astype(BF16))
        probs.append(jnp.concatenate(p_rows, axis=0))
    for g in range(SWA_KV_HEADS):
        acc = _mm(probs[g], vcat[:, head_cols(g)])
        for j in range(SWA_GROUP):
            h = g * SWA_GROUP + j
            o_ref[:, head_cols(h)] = acc[j * WINDOW:(j + 1) * WINDOW, :] / dens[h]


def _swa_prompt(proj, bsz, t_len, bias_tab, sink_tab):
    n_blk = t_len // WINDOW
    cur = lambda col: (lambda b, n: (b * n_blk + n, col))
    prev = lambda col: (lambda b, n: (b * n_blk + jnp.maximum(n - 1, 0), col))
    return pl.pallas_call(
        _swa_prompt_kernel,
        grid=(bsz, n_blk),
        in_specs=[pl.BlockSpec((WINDOW, SWA_Q_W), cur(0)),
                  pl.BlockSpec((WINDOW, SWA_KV_W), cur(4)), pl.BlockSpec((WINDOW, SWA_KV_W), prev(4)),
                  pl.BlockSpec((WINDOW, SWA_KV_W), cur(5)), pl.BlockSpec((WINDOW, SWA_KV_W), prev(5)),
                  pl.BlockSpec((SWA_HEADS, WINDOW, 2 * WINDOW), lambda b, n: (0, 0, 0)),
                  pl.BlockSpec((SWA_HEADS, LANES), lambda b, n: (0, 0))],
        out_specs=pl.BlockSpec((WINDOW, SWA_Q_W), lambda b, n: (b * n_blk + n, 0)),
        out_shape=jax.ShapeDtypeStruct((bsz * t_len, SWA_Q_W), F32),
        compiler_params=pltpu.CompilerParams(dimension_semantics=("parallel", "arbitrary"),
                                             vmem_limit_bytes=VMEM_LIMIT),
        name="swa_prompt",
    )(proj, proj, proj, proj, proj, bias_tab, sink_tab)


def _swa_sample_kernel(q_ref, kn_ref, vn_ref, kc_ref, vc_ref, bias_ref, bias0_ref, sink_ref, o_ref):
    lane = lax.broadcasted_iota(jnp.int32, (SWA_GROUP, WINDOW), 1)
    rnd = lambda a: a.astype(BF16).astype(F32)
    pairs = [(b, g) for b in range(DEC_TILE) for g in range(SWA_KV_HEADS)]
    heads = lambda g: slice(g * SWA_GROUP, (g + 1) * SWA_GROUP)
    kv_cols = lambda g: slice(g * SWA_HEAD_DIM, (g + 1) * SWA_HEAD_DIM)
    scores = [_mm_nt(q_ref[b, heads(g), :], kc_ref[b, :, kv_cols(g)]) for b, g in pairs]
    probs, own_p, dens = [], [], []
    for (b, g), s in zip(pairs, scores):
        logits = s * (SWA_HEAD_DIM ** -0.5) + bias_ref[heads(g), :]
        logits = jnp.where(lane >= 1, logits, MASKED)
        own = jnp.sum(rnd(q_ref[b, heads(g), :]) * rnd(kn_ref[b:b + 1, kv_cols(g)]), axis=-1,
                      keepdims=True) * (SWA_HEAD_DIM ** -0.5) + bias0_ref[heads(g), 0:1]
        sink = sink_ref[heads(g), 0:1]
        m = jnp.maximum(jnp.maximum(jnp.max(logits, axis=-1, keepdims=True), sink), own)
        p = jnp.exp(logits - m)
        pe = jnp.exp(own - m)
        probs.append(p)
        own_p.append(pe)
        dens.append(jnp.sum(p, axis=-1, keepdims=True) + pe + jnp.exp(sink - m))
    for (b, g), p, pe, den in zip(pairs, probs, own_p, dens):
        acc = _mm(p, vc_ref[b, :, kv_cols(g)]) + rnd(pe) * rnd(vn_ref[b:b + 1, kv_cols(g)])
        o_ref[b, heads(g), :] = acc / den


def _swa_sample(q3, kn, vn, kc, vc, bias_dec, bias0, sink_tab):
    n_dec = q3.shape[0]
    tab = lambda i: (0, 0)
    return pl.pallas_call(
        _swa_sample_kernel,
        grid=(n_dec // DEC_TILE,),
        in_specs=[pl.BlockSpec((DEC_TILE, SWA_HEADS, SWA_HEAD_DIM), lambda i: (i, 0, 0)),
                  pl.BlockSpec((DEC_TILE, SWA_KV_W), lambda i: (i, 0)),
                  pl.BlockSpec((DEC_TILE, SWA_KV_W), lambda i: (i, 0)),
                  pl.BlockSpec((DEC_TILE, WINDOW, SWA_KV_W), lambda i: (i, 0, 0)),
                  pl.BlockSpec((DEC_TILE, WINDOW, SWA_KV_W), lambda i: (i, 0, 0)),
                  pl.BlockSpec((SWA_HEADS, WINDOW), tab), pl.BlockSpec((SWA_HEADS, LANES), tab),
                  pl.BlockSpec((SWA_HEADS, LANES), tab)],
        out_specs=pl.BlockSpec((DEC_TILE, SWA_HEADS, SWA_HEAD_DIM), lambda i: (i, 0, 0)),
        out_shape=jax.ShapeDtypeStruct((n_dec, SWA_HEADS, SWA_HEAD_DIM), F32),
        compiler_params=pltpu.CompilerParams(dimension_semantics=("parallel",),
                                             vmem_limit_bytes=VMEM_LIMIT),
        name="swa_sample",
    )(q3, kn, vn, kc, vc, bias_dec, bias0, sink_tab)


def _t5_bucket(rel):
    exact = REL_BUCKETS // 2
    nf = jnp.maximum(rel, 1).astype(F32)
    large = exact + (jnp.log(nf / exact) / math.log(REL_MAX_DIST / exact)
                     * (REL_BUCKETS - exact)).astype(jnp.int32)
    return jnp.where(rel < exact, rel, jnp.minimum(large, REL_BUCKETS - 1))


def _lane_row(v, width=LANES):
    return jnp.zeros((1, width), F32).at[0, :v.shape[0]].set(v.astype(F32))


def kernel(x_prompt, x_sample, state_gdn, state_gdn_conv, state_lru, state_lru_conv, cache_swa_k, cache_swa_v, w_in_ab, gdn_conv_w, gdn_a_log, gdn_dt_bias, gdn_norm_w, lru_conv_w, lru_conv_b, lru_w_r, lru_b_r, lru_w_i, lru_b_i, lru_lam, w_out_ab, w_in_c, b_in_c, swa_sinks, w_out_c, b_out_c, rel_bias, ln_mix_g, ln_mix_b, ln_ffn_g, ln_ffn_b, peer_w_q, peer_keys, peer_u, peer_v):
    bsz, t_len, _ = x_prompt.shape
    n_dec = x_sample.shape[0]
    n_prompt = bsz * t_len
    assert x_sample.shape[1] == 1 and cache_swa_k.shape[2] == WINDOW

    def peer_ffn(x, layer):
        return _peer_layer(x, peer_w_q[layer].astype(BF16), peer_keys[layer].astype(BF16),
                           peer_u[layer].astype(BF16), _blocked_transpose(peer_v[layer]),
                           ln_ffn_g[layer], ln_ffn_b[layer])

    x = _tokens(x_prompt.reshape(n_prompt, D_MODEL), x_sample.reshape(n_dec, D_MODEL))
    (x1_p, x1_s, p_gdn, p_gdn_conv, p_lru, p_lru_conv, s_gdn, s_gdn_conv, s_lru,
     s_lru_conv) = _layer0_mixers(
        x, bsz, t_len, n_dec, state_gdn, state_gdn_conv, state_lru, state_lru_conv, w_in_ab,
        gdn_conv_w, gdn_a_log, gdn_dt_bias, gdn_norm_w, lru_conv_w, lru_conv_b, lru_w_r, lru_b_r,
        lru_w_i, lru_b_i, lru_lam, w_out_ab, ln_mix_g, ln_mix_b)
    x = peer_ffn(_tokens(x1_p, x1_s), 0)
    x1_p, x1_s, p_k, p_v, s_k, s_v = _layer1_mixers(
        x, bsz, t_len, n_dec, cache_swa_k, cache_swa_v, w_in_c, b_in_c, swa_sinks, w_out_c, b_out_c,
        rel_bias, ln_mix_g, ln_mix_b)
    x = peer_ffn(_tokens(x1_p, x1_s), 1)

    n_real = n_prompt + n_dec
    lead = lambda a: a[None]
    return (x[:n_prompt].reshape(bsz, t_len, D_MODEL), x[n_prompt:n_real].reshape(n_dec, 1, D_MODEL),
            lead(p_gdn), lead(p_gdn_conv), lead(p_lru.reshape(bsz, LRU_WIDTH)), lead(p_lru_conv),
            lead(p_k), lead(p_v),
            lead(s_gdn), lead(s_gdn_conv), lead(s_lru), lead(s_lru_conv), lead(s_k), lead(s_v))


def _blocked_transpose(v):
    n_exp, d = v.shape
    return jnp.swapaxes(v.reshape(n_exp // EXP_BLOCK, EXP_BLOCK, d), 1, 2).astype(BF16)


def _sequence_tails(rows, bsz, t_len, n):
    return jnp.stack([rows[(b + 1) * t_len - n:(b + 1) * t_len] for b in range(bsz)])


def _tokens(xp_rows, xs_rows):
    n_real = xp_rows.shape[0] + xs_rows.shape[0]
    n_tok = -(-n_real // TOK_TILE) * TOK_TILE
    return jnp.concatenate([xp_rows, xs_rows, jnp.zeros((n_tok - n_real, D_MODEL), F32)])


def _layer0_mixers(x, bsz, t_len, n_dec, state_gdn, state_gdn_conv, state_lru, state_lru_conv,
                   w_in_ab, gdn_conv_w, gdn_a_log, gdn_dt_bias, gdn_norm_w, lru_conv_w, lru_conv_b,
                   lru_w_r, lru_b_r, lru_w_i, lru_b_i, lru_lam, w_out_ab, ln_mix_g, ln_mix_b):
    n_prompt = bsz * t_len
    n_real = n_prompt + n_dec
    assert t_len % SEQ_TILE == 0 and n_dec % DEC_TILE == 0 and n_prompt % ROW_TILE == 0
    w_in = w_in_ab[0]
    c0 = GDN_CONV_CH + GDN_V_W
    c1 = c0 + 2 * GDN_HEADS
    w_all = jnp.concatenate([w_in[:, :c0], w_in[:, c1:], w_in[:, c0:c1],
                             jnp.zeros((D_MODEL, LANES - 2 * GDN_HEADS), F32)], axis=1).astype(BF16)
    proj = _project(x, w_all, jnp.zeros((w_all.shape[1],), F32))
    gdn_w = (gdn_conv_w[0], _lane_row(gdn_a_log[0]), _lane_row(gdn_dt_bias[0]),
             gdn_norm_w[0].reshape(1, GDN_DV))
    eye_b = jnp.eye(LRU_BLOCKS, dtype=F32)

    def block_diag(w):
        return (eye_b[:, None, :, None] * w[:, :, None, :]).reshape(LRU_WIDTH, LRU_WIDTH).astype(BF16)

    wide = lambda v: v.reshape(1, LRU_WIDTH)
    lru_w = (lru_conv_w[0], wide(lru_conv_b[0]), block_diag(lru_w_r[0]), wide(lru_b_r[0]),
             block_diag(lru_w_i[0]), wide(lru_b_i[0]), wide(lru_lam[0]))
    oa_p, p_gdn = _gdn_prompt(proj, bsz, t_len, *gdn_w)
    ob_p, p_lru = _lru_prompt(proj, bsz, t_len, lru_w)
    oa_s, ob_s, s_gdn, s_lru = _ab_sample(
        proj, n_prompt, n_dec, jnp.swapaxes(state_gdn_conv[0], 0, 1),
        jnp.swapaxes(state_lru_conv[0], 0, 1), state_gdn[0], state_lru[0], gdn_w, lru_w)
    w_out = w_out_ab[0].astype(BF16)
    zero_b = jnp.zeros((D_MODEL,), F32)
    x1_p = _outproj_ln(x, oa_p, 0, ob_p, 0, w_out, zero_b, ln_mix_g[0], ln_mix_b[0], ROW_TILE)
    x1_s = _outproj_ln(x[n_prompt:n_real], oa_s, 0, ob_s, 0, w_out, zero_b, ln_mix_g[0], ln_mix_b[0],
                       n_dec)
    pre = _sequence_tails(proj, bsz, t_len, CONV_W - 1)
    p_gdn_conv = pre[:, :, :GDN_CONV_CH]
    p_lru_conv = pre[:, :, c0:c0 + LRU_WIDTH]
    new = proj[n_prompt:n_real]
    s_gdn_conv = jnp.concatenate([state_gdn_conv[0][:, 1:], new[:, None, :GDN_CONV_CH]], axis=1)
    s_lru_conv = jnp.concatenate([state_lru_conv[0][:, 1:], new[:, None, c0:c0 + LRU_WIDTH]], axis=1)
    return x1_p, x1_s, p_gdn, p_gdn_conv, p_lru, p_lru_conv, s_gdn, s_gdn_conv, s_lru, s_lru_conv


def _layer1_mixers(x, bsz, t_len, n_dec, cache_swa_k, cache_swa_v, w_in_c, b_in_c, swa_sinks,
                   w_out_c, b_out_c, rel_bias, ln_mix_g, ln_mix_b):
    n_prompt = bsz * t_len
    n_real = n_prompt + n_dec
    assert t_len % WINDOW == 0 and n_dec % DEC_TILE == 0 and n_prompt % ROW_TILE == 0
    proj = _project(x, w_in_c[0].astype(BF16), b_in_c[0])
    rel = jnp.arange(WINDOW)[:, None] + WINDOW - jnp.arange(2 * WINDOW)[None, :]
    bias_vec = rel_bias.astype(F32)[_t5_bucket(jnp.arange(WINDOW))]
    pick = (jnp.clip(rel, 0, WINDOW - 1)[:, :, None] == jnp.arange(WINDOW)).astype(F32)
    bias_tab = jnp.einsum('qkd,dh->hqk', pick, bias_vec, precision=lax.Precision.HIGHEST)
    bias_dec = bias_vec[jnp.clip(WINDOW - jnp.arange(WINDOW), 0, WINDOW - 1)].T
    bias_own = jnp.broadcast_to(bias_vec[0][:, None], (SWA_HEADS, LANES))
    sink_tab = jnp.broadcast_to(swa_sinks[0].astype(F32)[:, None], (SWA_HEADS, LANES))
    attn_p = _swa_prompt(proj, bsz, t_len, bias_tab, sink_tab)
    new = proj[n_prompt:n_real]
    kn, vn = new[:, SWA_Q_W:SWA_Q_W + SWA_KV_W], new[:, SWA_Q_W + SWA_KV_W:]
    kc = cache_swa_k[0].reshape(n_dec, WINDOW, SWA_KV_W)
    vc = cache_swa_v[0].reshape(n_dec, WINDOW, SWA_KV_W)
    attn_s = _swa_sample(new[:, :SWA_Q_W].reshape(n_dec, SWA_HEADS, SWA_HEAD_DIM), kn, vn, kc, vc,
                         bias_dec, bias_own, sink_tab).reshape(n_dec, SWA_Q_W)
    w_out = w_out_c[0].astype(BF16)
    x1_p = _outproj_ln(x, attn_p, 0, attn_p, 1, w_out, b_out_c[0], ln_mix_g[1], ln_mix_b[1], ROW_TILE)
    x1_s = _outproj_ln(x[n_prompt:n_real], attn_s, 0, attn_s, 1, w_out, b_out_c[0], ln_mix_g[1],
                       ln_mix_b[1], n_dec)
    kv_p = _sequence_tails(proj, bsz, t_len, WINDOW)[:, :, SWA_Q_W:]
    heads = (SWA_KV_HEADS, SWA_HEAD_DIM)
    p_k = kv_p[:, :, :SWA_KV_W].reshape(bsz, WINDOW, *heads)
    p_v = kv_p[:, :, SWA_KV_W:].reshape(bsz, WINDOW, *heads)
    s_k = jnp.concatenate([kc[:, 1:], kn[:, None]], axis=1).reshape(n_dec, WINDOW, *heads)
    s_v = jnp.concatenate([vc[:, 1:], vn[:, None]], axis=1).reshape(n_dec, WINDOW, *heads)
    return x1_p, x1_s, p_k, p_v, s_k, s_v
```

```python
import functools
import math

import jax
import jax.numpy as jnp
from jax import lax
from jax.experimental import pallas as pl
from jax.experimental.pallas import tpu as pltpu

F32 = jnp.float32
BF16 = jnp.bfloat16

D_MODEL = 1024
DEPTH = 2
DN_ALPHA = (2 * DEPTH) ** 0.25
LN_EPS = 1e-5

PEER_HEADS = 8
PEER_N_KEYS = 128
PEER_D_SUB = 128
PEER_TOPK = 16
PEER_Q_W = PEER_HEADS * 2 * PEER_D_SUB

LANES = 128
SUBLANES = 8
VMEM_LIMIT = 56 * 1024 * 1024

TOK_TILE = 256
SUB_TILE = 256
EXP_BLOCK = 2048
MXU_PARTS = 8


def _oddeven_merge(lo, hi, r):
    step = r * 2
    if step < hi - lo:
        yield from _oddeven_merge(lo, hi, step)
        yield from _oddeven_merge(lo + r, hi, step)
        for i in range(lo + r, hi - r, step):
            yield (i, i + r)
    else:
        yield (lo, lo + r)


def _oddeven_sort_pairs(lo, hi):
    if hi - lo >= 1:
        mid = lo + (hi - lo) // 2
        yield from _oddeven_sort_pairs(lo, mid)
        yield from _oddeven_sort_pairs(mid + 1, hi)
        yield from _oddeven_merge(lo, hi, 1)


_SORT16 = tuple(_oddeven_sort_pairs(0, 15))


def _cmpx(v, i, j):
    a, b = v[i], v[j]
    if b is None:
        return
    if a is None:
        v[i], v[j] = b, None
        return
    v[i], v[j] = jnp.maximum(a, b), jnp.minimum(a, b)


def _sort16_desc(v):
    v = list(v)
    for i, j in _SORT16:
        _cmpx(v, i, j)
    return v


def _merge_top16(x, y):
    v = []
    for k in range(16):
        a, b = x[k], y[15 - k]
        v.append(b if a is None else a if b is None else jnp.maximum(a, b))
    for d in (8, 4, 2, 1):
        for i in range(16):
            if not i & d:
                _cmpx(v, i, i + d)
    return v


def _top16_of_keys(s_t):
    v = _sort16_desc([s_t[SUBLANES * r:SUBLANES * (r + 1), :] for r in range(16)])
    for shift in (4, 2, 1):
        v = _merge_top16(v, [pltpu.roll(a, shift, 0) for a in v])
    return v


_PAIR_ROWS = [[(k, l) for l in range(PEER_TOPK // (k + 1))] for k in range(PEER_TOPK)]


def _top16_pair_sums(a, b):
    top = [a[0] + b[l] for l in range(16)]
    rest = [a[k] + b[l] for row in _PAIR_ROWS[1:] for (k, l) in row]
    for lo in range(0, len(rest), 16):
        chunk = rest[lo:lo + 16]
        chunk = chunk + [None] * (16 - len(chunk))
        top = _merge_top16(top, _sort16_desc(chunk))
    return top


def _peer_select_kernel(x_ref, wq_ref, keys_ref, xt_ref, thr_ref, a1_ref, s2_ref, b2_ref,
                        s1_scr, top_scr, stat_scr):
    x = x_ref[...]
    xt_ref[...] = x.T.astype(BF16)
    q = jnp.dot(x.astype(BF16), wq_ref[...], preferred_element_type=F32).astype(BF16)
    nt = (((1,), (1,)), ((), ()))
    sub = lambda hc: slice(hc * PEER_D_SUB, (hc + 1) * PEER_D_SUB)
    chunks = [slice(tc * LANES, (tc + 1) * LANES) for tc in range(TOK_TILE // LANES)]
    for h in range(PEER_HEADS):
        s1_scr[h] = lax.dot_general(keys_ref[h, 0], q[:, sub(2 * h)], nt, preferred_element_type=F32)
        s2 = lax.dot_general(keys_ref[h, 1], q[:, sub(2 * h + 1)], nt, preferred_element_type=F32)
        for tc, lanes in enumerate(chunks):
            s2_ref[h, tc] = s2[:, lanes]
    for h in range(PEER_HEADS):
        for tc, lanes in enumerate(chunks):
            for c, s_t in ((0, s1_scr[h, :, lanes]), (1, s2_ref[h, tc])):
                for k, v in enumerate(_top16_of_keys(s_t)):
                    top_scr[c, k, h:h + 1, lanes] = v[0:1, :]
    for lanes in chunks:
        a = [top_scr[0, k, :, lanes] for k in range(PEER_TOPK)]
        b = [top_scr[1, k, :, lanes] for k in range(PEER_TOPK)]
        top = _top16_pair_sums(a, b)
        m = a[0] + b[0]
        z = jnp.exp(top[0] - m)
        for k in range(1, PEER_TOPK):
            z = z + jnp.exp(top[k] - m)
        stat_scr[0, :, lanes] = top[PEER_TOPK - 1]
        stat_scr[1, :, lanes] = 1.0 / z
    for h in range(PEER_HEADS):
        row = lambda ref, i: ref[i, h:h + 1, :]
        s1 = s1_scr[h]
        thr_ref[h] = row(stat_scr, 0) - s1
        a1_ref[h] = jnp.exp(s1 - top_scr[0, 0, h:h + 1, :]) * row(stat_scr, 1)
        for tc, lanes in enumerate(chunks):
            b2_ref[h, tc] = jnp.exp(s2_ref[h, tc] - top_scr[1, 0, h:h + 1, lanes])


def _gelu_tanh(x):
    return 0.5 * x * (1.0 + jnp.tanh(math.sqrt(2.0 / math.pi) * (x + 0.044715 * (x * x * x))))


def _peer_dense_kernel(xt_ref, thr_ref, a1_ref, s2_ref, b2_ref, u_ref, vt_ref, x_ref, g_ref, beta_ref,
                       o_ref, yt_ref, act_scr, w_scr):
    e = pl.program_id(1)
    part_rows = EXP_BLOCK // MXU_PARTS

    @pl.when(e == 0)
    def _():
        yt_ref[...] = jnp.zeros_like(yt_ref)

    def gate_block(r, tc, off):
        local = slice(tc * LANES, (tc + 1) * LANES)
        lanes = slice(off + tc * LANES, off + (tc + 1) * LANES)
        chunk = off // LANES + tc
        rows = slice(r * PEER_N_KEYS, (r + 1) * PEER_N_KEYS)
        w = jnp.zeros((PEER_N_KEYS, LANES), F32)
        for h in range(PEER_HEADS):
            thr = thr_ref[h, r:r + 1, lanes]
            a1 = a1_ref[h, r:r + 1, lanes]
            w = w + jnp.where(s2_ref[h, chunk] >= thr, b2_ref[h, chunk], 0.0) * a1
        w_scr[rows, local] = (w * _gelu_tanh(act_scr[rows, local])).astype(BF16)

    for off in range(0, TOK_TILE, SUB_TILE):
        cols = slice(off, off + SUB_TILE)
        for part in range(MXU_PARTS):
            ra = slice(part * part_rows, (part + 1) * part_rows)
            act_scr[ra, :] = jnp.dot(u_ref[ra, :], xt_ref[:, cols], preferred_element_type=F32)
            for r in range(part * part_rows // PEER_N_KEYS, (part + 1) * part_rows // PEER_N_KEYS):
                for tc in range(SUB_TILE // LANES):
                    gate_block(r, tc, off)
            yt_ref[:, cols] += jnp.dot(vt_ref[0, :, ra], w_scr[ra, :], preferred_element_type=F32)

    @pl.when(e == pl.num_programs(1) - 1)
    def _():
        o_ref[...] = _layer_norm(DN_ALPHA * x_ref[...] + yt_ref[...].T, g_ref[...], beta_ref[...])


def _peer_layer(x, wq, keys, u, vt, ln_g, ln_b):
    n_tok = x.shape[0]
    n_tiles = n_tok // TOK_TILE
    n_exp = u.shape[0]
    fac = jax.ShapeDtypeStruct((PEER_HEADS, PEER_N_KEYS, n_tok), F32)
    fac_spec = pl.BlockSpec((PEER_HEADS, PEER_N_KEYS, TOK_TILE), lambda t: (0, 0, t))
    slab = jax.ShapeDtypeStruct((PEER_HEADS, n_tok // LANES, PEER_N_KEYS, LANES), F32)
    slab_block = (PEER_HEADS, TOK_TILE // LANES, PEER_N_KEYS, LANES)
    xt, thr, a1, s2, b2 = pl.pallas_call(
        _peer_select_kernel,
        grid=(n_tiles,),
        in_specs=[
            pl.BlockSpec((TOK_TILE, D_MODEL), lambda t: (t, 0)),
            pl.BlockSpec((D_MODEL, PEER_Q_W), lambda t: (0, 0)),
            pl.BlockSpec((PEER_HEADS, 2, PEER_N_KEYS, PEER_D_SUB), lambda t: (0, 0, 0, 0)),
        ],
        out_specs=[pl.BlockSpec((D_MODEL, TOK_TILE), lambda t: (0, t)), fac_spec, fac_spec,
                   pl.BlockSpec(slab_block, lambda t: (0, t, 0, 0)),
                   pl.BlockSpec(slab_block, lambda t: (0, t, 0, 0))],
        out_shape=[jax.ShapeDtypeStruct((D_MODEL, n_tok), BF16), fac, fac, slab, slab],
        scratch_shapes=[pltpu.VMEM((PEER_HEADS, PEER_N_KEYS, TOK_TILE), F32),
                        pltpu.VMEM((2, PEER_TOPK, PEER_HEADS, TOK_TILE), F32),
                        pltpu.VMEM((2, PEER_HEADS, TOK_TILE), F32)],
        compiler_params=pltpu.CompilerParams(dimension_semantics=("parallel",),
                                             vmem_limit_bytes=VMEM_LIMIT),
        name="peer_select",
    )(x, wq, keys)

    fac_spec2 = pl.BlockSpec(slab_block, lambda t, e: (0, t, 0, 0))
    row_spec = pl.BlockSpec((PEER_HEADS, EXP_BLOCK // PEER_N_KEYS, TOK_TILE), lambda t, e: (0, e, t))
    ln_row = pl.BlockSpec((1, D_MODEL), lambda t, e: (0, 0))
    return pl.pallas_call(
        _peer_dense_kernel,
        grid=(n_tiles, n_exp // EXP_BLOCK),
        in_specs=[pl.BlockSpec((D_MODEL, TOK_TILE), lambda t, e: (0, t)),
                  row_spec, row_spec, fac_spec2, fac_spec2,
                  pl.BlockSpec((EXP_BLOCK, D_MODEL), lambda t, e: (e, 0)),
                  pl.BlockSpec((1, D_MODEL, EXP_BLOCK), lambda t, e: (e, 0, 0)),
                  pl.BlockSpec((TOK_TILE, D_MODEL), lambda t, e: (t, 0)), ln_row, ln_row],
        out_specs=pl.BlockSpec((TOK_TILE, D_MODEL), lambda t, e: (t, 0)),
        out_shape=jax.ShapeDtypeStruct((n_tok, D_MODEL), F32),
        scratch_shapes=[pltpu.VMEM((D_MODEL, TOK_TILE), F32),
                        pltpu.VMEM((EXP_BLOCK, SUB_TILE), F32), pltpu.VMEM((EXP_BLOCK, SUB_TILE), BF16)],
        compiler_params=pltpu.CompilerParams(dimension_semantics=("parallel", "arbitrary"),
                                             vmem_limit_bytes=VMEM_LIMIT),
        name="peer_dense",
    )(xt, thr, a1, s2, b2, u, vt, x, ln_g.reshape(1, D_MODEL), ln_b.reshape(1, D_MODEL))


GDN_HEADS = 4
GDN_DK = 128
GDN_DV = 128
GDN_CHUNK = 64
CONV_W = 4
GDN_QK_W = GDN_HEADS * GDN_DK
GDN_V_W = GDN_HEADS * GDN_DV
GDN_CONV_CH = 2 * GDN_QK_W + GDN_V_W
LRU_WIDTH = 512
LRU_BLOCKS = 8
LRU_C = 8.0
SWA_HEADS = 16
SWA_KV_HEADS = 4
SWA_HEAD_DIM = 64
SWA_GROUP = SWA_HEADS // SWA_KV_HEADS
SWA_Q_W = SWA_HEADS * SWA_HEAD_DIM
SWA_KV_W = SWA_KV_HEADS * SWA_HEAD_DIM
WINDOW = 128
REL_BUCKETS = 32
REL_MAX_DIST = 128
MASKED = -1e30

SEQ_TILE = 256
ROW_TILE = 256
DEC_TILE = 8

_NT = (((1,), (1,)), ((), ()))
_TN = (((0,), (0,)), ((), ()))


def _mm(a, b):
    return jnp.dot(a.astype(BF16), b.astype(BF16), preferred_element_type=F32)


def _mm_nt(a, b):
    return lax.dot_general(a.astype(BF16), b.astype(BF16), _NT, preferred_element_type=F32)


def _mm_tn(a, b):
    return lax.dot_general(a.astype(BF16), b.astype(BF16), _TN, preferred_element_type=F32)


def _mm_split(a, b):
    a_hi = a.astype(BF16)
    b_hi = b.astype(BF16)
    a_lo = (a - a_hi.astype(F32)).astype(BF16)
    b_lo = (b - b_hi.astype(F32)).astype(BF16)
    dot = functools.partial(jnp.dot, preferred_element_type=F32)
    return dot(a_hi, b_hi) + (dot(a_hi, b_lo) + dot(a_lo, b_hi))


def _sigmoid(x):
    return 1.0 / (1.0 + jnp.exp(-x))


def _silu(x):
    return x * _sigmoid(x)


def _softplus(x):
    return jnp.maximum(x, 0.0) + jnp.log1p(jnp.exp(-jnp.abs(x)))


def _layer_norm(z, g, b):
    mu = jnp.mean(z, axis=-1, keepdims=True)
    zc = z - mu
    var = jnp.mean(zc * zc, axis=-1, keepdims=True)
    return zc * lax.rsqrt(var + LN_EPS) * g + b


def _l2_normalize(x):
    return x * lax.rsqrt(jnp.sum(x * x, axis=-1, keepdims=True) + 1e-6)


def _causal_conv(x, xp_scr, w_ref, n_rows):
    xp_scr[8:8 + n_rows, :] = x
    y = x * w_ref[CONV_W - 1:CONV_W, :]
    for i in range(CONV_W - 1):
        y = y + xp_scr[5 + i:5 + i + n_rows, :] * w_ref[i:i + 1, :]
    xp_scr[0:8, :] = x[n_rows - 8:n_rows, :]
    return y


def _gdn_gates(ab, alog_row, dtb_row):
    g = -jnp.exp(alog_row) * _softplus(ab + dtb_row)
    return g, _sigmoid(ab)


def _gdn_out_norm(o, z, nw_row):
    o = o * lax.rsqrt(jnp.mean(o * o, axis=-1, keepdims=True) + 1e-6) * nw_row
    return o * _silu(z)


def _lru_coeffs(xc, wr_ref, br_ref, wi_ref, bi_ref, lam_ref):
    r = _sigmoid(_mm(xc, wr_ref[...]) + br_ref[...])
    i = _sigmoid(_mm(xc, wi_ref[...]) + bi_ref[...])
    log_a = -LRU_C * r * _softplus(-lam_ref[...])
    a = jnp.exp(log_a)
    one_minus_a2 = -jnp.tanh(log_a) * (a * a + 1.0)
    return a, jnp.sqrt(one_minus_a2) * (i * xc)


def _proj_kernel(x_ref, w_ref, b_ref, o_ref):
    o_ref[...] = _mm(x_ref[...], w_ref[...]) + b_ref[...]


def _project(x, w, b):
    n, k = x.shape
    m = w.shape[1]
    return pl.pallas_call(
        _proj_kernel,
        grid=(n // ROW_TILE,),
        in_specs=[pl.BlockSpec((ROW_TILE, k), lambda t: (t, 0)),
                  pl.BlockSpec((k, m), lambda t: (0, 0)),
                  pl.BlockSpec((1, m), lambda t: (0, 0))],
        out_specs=pl.BlockSpec((ROW_TILE, m), lambda t: (t, 0)),
        out_shape=jax.ShapeDtypeStruct((n, m), F32),
        compiler_params=pltpu.CompilerParams(dimension_semantics=("parallel",),
                                             vmem_limit_bytes=VMEM_LIMIT),
        name="proj",
    )(x, w, b.reshape(1, m))


def _outproj_ln_kernel(x_ref, oa_ref, ob_ref, w_ref, b_ref, g_ref, beta_ref, o_ref):
    half = oa_ref.shape[1]
    y = _mm(oa_ref[...], w_ref[0:half, :]) + _mm(ob_ref[...], w_ref[half:2 * half, :]) + b_ref[...]
    o_ref[...] = _layer_norm(DN_ALPHA * x_ref[...] + y, g_ref[...], beta_ref[...])


def _outproj_ln(x, oa, oa_col, ob, ob_col, w, b, g, beta, tile):
    n = oa.shape[0]
    half = w.shape[0] // 2
    row = lambda t: (0, 0)
    return pl.pallas_call(
        _outproj_ln_kernel,
        grid=(n // tile,),
        in_specs=[pl.BlockSpec((tile, D_MODEL), lambda t: (t, 0)),
                  pl.BlockSpec((tile, half), lambda t: (t, oa_col)),
                  pl.BlockSpec((tile, half), lambda t: (t, ob_col)),
                  pl.BlockSpec((2 * half, D_MODEL), row),
                  pl.BlockSpec((1, D_MODEL), row), pl.BlockSpec((1, D_MODEL), row),
                  pl.BlockSpec((1, D_MODEL), row)],
        out_specs=pl.BlockSpec((tile, D_MODEL), lambda t: (t, 0)),
        out_shape=jax.ShapeDtypeStruct((n, D_MODEL), F32),
        compiler_params=pltpu.CompilerParams(dimension_semantics=("parallel",),
                                             vmem_limit_bytes=VMEM_LIMIT),
        name="outproj_ln",
    )(x, oa, ob, w, b.reshape(1, D_MODEL), g.reshape(1, D_MODEL), beta.reshape(1, D_MODEL))


def _gdn_prompt_kernel(qkv_ref, ab_ref, z_ref, cw_ref, alog_ref, dtb_ref, nw_ref, o_ref, s_out_ref,
                       xp_scr, s_scr, q_scr, k_scr, v_scr, gc_scr, beta_scr, u_scr, w_scr, qk_scr):
    t = pl.program_id(1)

    @pl.when(t == 0)
    def _():
        xp_scr[0:8, :] = jnp.zeros((8, GDN_CONV_CH), F32)
        s_scr[...] = jnp.zeros_like(s_scr)

    y = _silu(_causal_conv(qkv_ref[...], xp_scr, cw_ref, SEQ_TILE))
    for h in range(GDN_HEADS):
        cols = slice(h * GDN_DK, (h + 1) * GDN_DK)
        q_scr[:, cols] = _l2_normalize(y[:, cols]) * (GDN_DK ** -0.5)
        k_scr[:, cols] = _l2_normalize(y[:, GDN_QK_W + h * GDN_DK:GDN_QK_W + (h + 1) * GDN_DK])
    v_scr[...] = y[:, 2 * GDN_QK_W:]
    g, beta = _gdn_gates(ab_ref[...], alog_ref[...], dtb_ref[...])
    beta_scr[...] = beta
    pos = lax.broadcasted_iota(jnp.int32, g.shape, 0) % GDN_CHUNK
    shift = 1
    while shift < GDN_CHUNK:
        g = g + jnp.where(pos >= shift, pltpu.roll(g, shift, 0), 0.0)
        shift *= 2
    gc_scr[...] = g

    ri = lax.broadcasted_iota(jnp.int32, (GDN_CHUNK, GDN_CHUNK), 0)
    ci = lax.broadcasted_iota(jnp.int32, (GDN_CHUNK, GDN_CHUNK), 1)
    eye = ri == ci
    incl = ri >= ci
    strict = ri > ci

    n_chunks = SEQ_TILE // GDN_CHUNK
    pairs = [(c, h) for c in range(n_chunks) for h in range(GDN_HEADS)]
    rows_of = lambda c: slice(c * GDN_CHUNK, (c + 1) * GDN_CHUNK)
    cols_of = lambda h: slice(h * GDN_DK, (h + 1) * GDN_DK)
    gcol_of = lambda c, h: gc_scr[rows_of(c), h:h + 1]

    pw, inv = [], []
    for c, h in pairs:
        k = k_scr[rows_of(c), cols_of(h)]
        gcol = gcol_of(c, h)
        bcol = beta_scr[rows_of(c), GDN_HEADS + h:GDN_HEADS + h + 1]
        grow = jnp.sum(jnp.where(eye, gcol, 0.0), axis=0, keepdims=True)
        decay = jnp.where(incl, jnp.exp(jnp.where(incl, gcol - grow, 0.0)), 0.0)
        kb = k * bcol
        neg = -jnp.where(strict, _mm_nt(kb, k) * decay, 0.0)
        qk_scr[rows_of(c), h * GDN_CHUNK:(h + 1) * GDN_CHUNK] = jnp.where(
            incl, _mm_nt(q_scr[rows_of(c), cols_of(h)], k) * decay, 0.0)
        pw.append(neg)
        inv.append(jnp.where(eye, 1.0, 0.0) + neg)
    for _ in range(5):
        pw = [_mm_split(p, p) for p in pw]
        inv = [a + _mm_split(a, p) for a, p in zip(inv, pw)]
    for (c, h), a in zip(pairs, inv):
        bcol = beta_scr[rows_of(c), GDN_HEADS + h:GDN_HEADS + h + 1]
        kb = k_scr[rows_of(c), cols_of(h)] * bcol
        u_scr[rows_of(c), cols_of(h)] = _mm_split(a, v_scr[rows_of(c), cols_of(h)] * bcol)
        w_scr[rows_of(c), cols_of(h)] = _mm_split(a, kb * jnp.exp(gcol_of(c, h)))

    state = [s_scr[h] for h in range(GDN_HEADS)]
    for c in range(n_chunks):
        rows = rows_of(c)
        v_new = [u_scr[rows, cols_of(h)] - _mm(w_scr[rows, cols_of(h)], state[h])
                 for h in range(GDN_HEADS)]
        for h in range(GDN_HEADS):
            cols = cols_of(h)
            gcol = gcol_of(c, h)
            o = (_mm(q_scr[rows, cols] * jnp.exp(gcol), state[h])
                 + _mm(qk_scr[rows, h * GDN_CHUNK:(h + 1) * GDN_CHUNK], v_new[h]))
            g_last = gcol[GDN_CHUNK - 1:GDN_CHUNK, :]
            state[h] = (state[h] * jnp.exp(g_last)
                        + _mm_tn(k_scr[rows, cols] * jnp.exp(g_last - gcol), v_new[h]))
            o_ref[rows, cols] = _gdn_out_norm(o, z_ref[rows, cols], nw_ref[...])
    for h in range(GDN_HEADS):
        s_scr[h] = state[h]

    @pl.when(t == pl.num_programs(1) - 1)
    def _():
        s_out_ref[0] = s_scr[...]


def _gdn_prompt(proj, bsz, t_len, cw, alog_row, dtb_row, nw_row):
    n_t = t_len // SEQ_TILE
    row = lambda b, t: (0, 0)
    return pl.pallas_call(
        _gdn_prompt_kernel,
        grid=(bsz, n_t),
        in_specs=[pl.BlockSpec((SEQ_TILE, GDN_CONV_CH), lambda b, t: (b * n_t + t, 0)),
                  pl.BlockSpec((SEQ_TILE, LANES), lambda b, t: (b * n_t + t, 24)),
                  pl.BlockSpec((SEQ_TILE, GDN_V_W), lambda b, t: (b * n_t + t, 3)),
                  pl.BlockSpec((CONV_W, GDN_CONV_CH), row),
                  pl.BlockSpec((1, LANES), row), pl.BlockSpec((1, LANES), row),
                  pl.BlockSpec((1, GDN_DV), row)],
        out_specs=[pl.BlockSpec((SEQ_TILE, GDN_V_W), lambda b, t: (b * n_t + t, 0)),
                   pl.BlockSpec((1, GDN_HEADS, GDN_DK, GDN_DV), lambda b, t: (b, 0, 0, 0))],
        out_shape=[jax.ShapeDtypeStruct((bsz * t_len, GDN_V_W), F32),
                   jax.ShapeDtypeStruct((bsz, GDN_HEADS, GDN_DK, GDN_DV), F32)],
        scratch_shapes=[pltpu.VMEM((SEQ_TILE + 8, GDN_CONV_CH), F32),
                        pltpu.VMEM((GDN_HEADS, GDN_DK, GDN_DV), F32),
                        pltpu.VMEM((SEQ_TILE, GDN_QK_W), F32), pltpu.VMEM((SEQ_TILE, GDN_QK_W), F32),
                        pltpu.VMEM((SEQ_TILE, GDN_V_W), F32),
                        pltpu.VMEM((SEQ_TILE, LANES), F32), pltpu.VMEM((SEQ_TILE, LANES), F32),
                        pltpu.VMEM((SEQ_TILE, GDN_V_W), F32), pltpu.VMEM((SEQ_TILE, GDN_QK_W), F32),
                        pltpu.VMEM((SEQ_TILE, GDN_HEADS * GDN_CHUNK), F32)],
        compiler_params=pltpu.CompilerParams(dimension_semantics=("parallel", "arbitrary"),
                                             vmem_limit_bytes=VMEM_LIMIT),
        name="gdn_prompt",
    )(proj, proj, proj, cw, alog_row, dtb_row, nw_row)


def _lru_prompt_kernel(xr_ref, gate_ref, cw_ref, cb_ref, wr_ref, br_ref, wi_ref, bi_ref, lam_ref,
                       o_ref, h_out_ref, xp_scr, h_scr):
    t = pl.program_id(1)

    @pl.when(t == 0)
    def _():
        xp_scr[0:8, :] = jnp.zeros((8, LRU_WIDTH), F32)
        h_scr[...] = jnp.zeros_like(h_scr)

    xc = _causal_conv(xr_ref[...], xp_scr, cw_ref, SEQ_TILE) + cb_ref[...]
    a, b = _lru_coeffs(xc, wr_ref, br_ref, wi_ref, bi_ref, lam_ref)
    pos = lax.broadcasted_iota(jnp.int32, a.shape, 0)
    shift = 1
    while shift < SEQ_TILE:
        valid = pos >= shift
        b = jnp.where(valid, a * pltpu.roll(b, shift, 0) + b, b)
        a = jnp.where(valid, a * pltpu.roll(a, shift, 0), a)
        shift *= 2
    h = b + a * h_scr[...]
    h_scr[...] = h[SEQ_TILE - 1:SEQ_TILE, :]
    o_ref[...] = jax.nn.gelu(gate_ref[...]) * h

    @pl.when(t == pl.num_programs(1) - 1)
    def _():
        h_out_ref[0] = h[SEQ_TILE - 1:SEQ_TILE, :]


def _lru_prompt(proj, bsz, t_len, lru_w):
    n_t = t_len // SEQ_TILE
    row = lambda b, t: (0, 0)
    wide = pl.BlockSpec((1, LRU_WIDTH), row)
    sq = pl.BlockSpec((LRU_WIDTH, LRU_WIDTH), row)
    return pl.pallas_call(
        _lru_prompt_kernel,
        grid=(bsz, n_t),
        in_specs=[pl.BlockSpec((SEQ_TILE, LRU_WIDTH), lambda b, t: (b * n_t + t, 4)),
                  pl.BlockSpec((SEQ_TILE, LRU_WIDTH), lambda b, t: (b * n_t + t, 5)),
                  pl.BlockSpec((CONV_W, LRU_WIDTH), row), wide, sq, wide, sq, wide, wide],
        out_specs=[pl.BlockSpec((SEQ_TILE, LRU_WIDTH), lambda b, t: (b * n_t + t, 0)),
                   pl.BlockSpec((1, 1, LRU_WIDTH), lambda b, t: (b, 0, 0))],
        out_shape=[jax.ShapeDtypeStruct((bsz * t_len, LRU_WIDTH), F32),
                   jax.ShapeDtypeStruct((bsz, 1, LRU_WIDTH), F32)],
        scratch_shapes=[pltpu.VMEM((SEQ_TILE + 8, LRU_WIDTH), F32), pltpu.VMEM((1, LRU_WIDTH), F32)],
        compiler_params=pltpu.CompilerParams(dimension_semantics=("parallel", "arbitrary"),
                                             vmem_limit_bytes=VMEM_LIMIT),
        name="lru_prompt",
    )(proj, proj, *lru_w)


def _ab_sample_kernel(qkv_ref, ab_ref, z_ref, xr_ref, gate_ref, gbuf_ref, lbuf_ref, s_ref, h0_ref,
                      cw_ref, alog_ref, dtb_ref, nw_ref,
                      lcw_ref, lcb_ref, wr_ref, br_ref, wi_ref, bi_ref, lam_ref,
                      oa_ref, ob_ref, s_out_ref, h_out_ref, o_scr):
    y = qkv_ref[...] * cw_ref[CONV_W - 1:CONV_W, :]
    for i in range(CONV_W - 1):
        y = y + gbuf_ref[i] * cw_ref[i:i + 1, :]
    y = _silu(y)
    g, beta = _gdn_gates(ab_ref[...], alog_ref[...], dtb_ref[...])
    eg = jnp.exp(g)
    for h in range(GDN_HEADS):
        cols = slice(h * GDN_DK, (h + 1) * GDN_DK)
        q = _l2_normalize(y[:, cols]) * (GDN_DK ** -0.5)
        k = _l2_normalize(y[:, GDN_QK_W + h * GDN_DK:GDN_QK_W + (h + 1) * GDN_DK])
        v = y[:, 2 * GDN_QK_W + h * GDN_DV:2 * GDN_QK_W + (h + 1) * GDN_DV]
        qk = jnp.sum(q * k, axis=-1, keepdims=True)
        q_t = q.T
        k_t = k.T
        for b in range(DEC_TILE):
            s = s_ref[b, h]
            kcol = k_t[:, b:b + 1]
            e = eg[b:b + 1, h:h + 1]
            ks = jnp.sum(s * kcol, axis=0, keepdims=True)
            qs = jnp.sum(s * q_t[:, b:b + 1], axis=0, keepdims=True)
            v_new = beta[b:b + 1, GDN_HEADS + h:GDN_HEADS + h + 1] * (v[b:b + 1, :] - e * ks)
            o_scr[b:b + 1, cols] = e * qs + qk[b:b + 1, :] * v_new
            s_out_ref[b, h] = e * s + kcol * v_new
    for h in range(GDN_HEADS):
        cols = slice(h * GDN_DV, (h + 1) * GDN_DV)
        oa_ref[:, cols] = _gdn_out_norm(o_scr[:, cols], z_ref[:, cols], nw_ref[...])
    xr = xr_ref[...]
    xc = xr * lcw_ref[CONV_W - 1:CONV_W, :] + lcb_ref[...]
    for i in range(CONV_W - 1):
        xc = xc + lbuf_ref[i] * lcw_ref[i:i + 1, :]
    a, bb = _lru_coeffs(xc, wr_ref, br_ref, wi_ref, bi_ref, lam_ref)
    hid = a * h0_ref[...] + bb
    h_out_ref[...] = hid
    ob_ref[...] = jax.nn.gelu(gate_ref[...]) * hid


def _ab_sample(proj, row0, n_dec, gbuf_t, lbuf_t, s0, h0, gdn_w, lru_w):
    blk0 = row0 // DEC_TILE
    row = lambda i: (0, 0)
    wide = pl.BlockSpec((1, LRU_WIDTH), row)
    sq = pl.BlockSpec((LRU_WIDTH, LRU_WIDTH), row)
    return pl.pallas_call(
        _ab_sample_kernel,
        grid=(n_dec // DEC_TILE,),
        in_specs=[pl.BlockSpec((DEC_TILE, GDN_CONV_CH), lambda i: (blk0 + i, 0)),
                  pl.BlockSpec((DEC_TILE, LANES), lambda i: (blk0 + i, 24)),
                  pl.BlockSpec((DEC_TILE, GDN_V_W), lambda i: (blk0 + i, 3)),
                  pl.BlockSpec((DEC_TILE, LRU_WIDTH), lambda i: (blk0 + i, 4)),
                  pl.BlockSpec((DEC_TILE, LRU_WIDTH), lambda i: (blk0 + i, 5)),
                  pl.BlockSpec((CONV_W - 1, DEC_TILE, GDN_CONV_CH), lambda i: (0, i, 0)),
                  pl.BlockSpec((CONV_W - 1, DEC_TILE, LRU_WIDTH), lambda i: (0, i, 0)),
                  pl.BlockSpec((DEC_TILE, GDN_HEADS, GDN_DK, GDN_DV), lambda i: (i, 0, 0, 0)),
                  pl.BlockSpec((DEC_TILE, LRU_WIDTH), lambda i: (i, 0)),
                  pl.BlockSpec((CONV_W, GDN_CONV_CH), row),
                  pl.BlockSpec((1, LANES), row), pl.BlockSpec((1, LANES), row),
                  pl.BlockSpec((1, GDN_DV), row),
                  pl.BlockSpec((CONV_W, LRU_WIDTH), row), wide, sq, wide, sq, wide, wide],
        out_specs=[pl.BlockSpec((DEC_TILE, GDN_V_W), lambda i: (i, 0)),
                   pl.BlockSpec((DEC_TILE, LRU_WIDTH), lambda i: (i, 0)),
                   pl.BlockSpec((DEC_TILE, GDN_HEADS, GDN_DK, GDN_DV), lambda i: (i, 0, 0, 0)),
                   pl.BlockSpec((DEC_TILE, LRU_WIDTH), lambda i: (i, 0))],
        out_shape=[jax.ShapeDtypeStruct((n_dec, GDN_V_W), F32),
                   jax.ShapeDtypeStruct((n_dec, LRU_WIDTH), F32),
                   jax.ShapeDtypeStruct((n_dec, GDN_HEADS, GDN_DK, GDN_DV), F32),
                   jax.ShapeDtypeStruct((n_dec, LRU_WIDTH), F32)],
        scratch_shapes=[pltpu.VMEM((DEC_TILE, GDN_V_W), F32)],
        compiler_params=pltpu.CompilerParams(dimension_semantics=("parallel",),
                                             vmem_limit_bytes=VMEM_LIMIT),
        name="ab_sample",
    )(proj, proj, proj, proj, proj, gbuf_t, lbuf_t, s0, h0, *gdn_w, *lru_w)


def _swa_prompt_kernel(q_ref, kc_ref, kp_ref, vc_ref, vp_ref, bias_ref, sink_ref, o_ref):
    n = pl.program_id(1)
    kcat = jnp.concatenate([kp_ref[...], kc_ref[...]], axis=0)
    vcat = jnp.concatenate([vp_ref[...], vc_ref[...]], axis=0)
    qi = lax.broadcasted_iota(jnp.int32, (WINDOW, 2 * WINDOW), 0)
    ki = lax.broadcasted_iota(jnp.int32, (WINDOW, 2 * WINDOW), 1)
    rel = qi + WINDOW - ki
    mask = (rel >= 0) & (rel < WINDOW) & ((ki >= WINDOW) | (n > 0))
    head_cols = lambda h: slice(h * SWA_HEAD_DIM, (h + 1) * SWA_HEAD_DIM)
    scores = [_mm_nt(jnp.concatenate([q_ref[:, head_cols(g * SWA_GROUP + j)]
                                      for j in range(SWA_GROUP)], axis=0), kcat[:, head_cols(g)])
              for g in range(SWA_KV_HEADS)]
    probs, dens = [], []
    for g in range(SWA_KV_HEADS):
        p_rows = []
        for j in range(SWA_GROUP):
            h = g * SWA_GROUP + j
            logits = scores[g][j * WINDOW:(j + 1) * WINDOW, :] * (SWA_HEAD_DIM ** -0.5) + bias_ref[h]
            logits = jnp.where(mask, logits, MASKED)
            sink = sink_ref[h:h + 1, 0:1]
            m = jnp.maximum(jnp.max(logits, axis=-1, keepdims=True), sink)
            p = jnp.exp(logits - m)
            dens.append(jnp.sum(p, axis=-1, keepdims=True) + jnp.exp(sink - m))
            p_rows.append(p.astype(BF16))
        probs.append(jnp.concatenate(p_rows, axis=0))
    for g in range(SWA_KV_HEADS):
        acc = _mm(probs[g], vcat[:, head_cols(g)])
        for j in range(SWA_GROUP):
            h = g * SWA_GROUP + j
            o_ref[:, head_cols(h)] = acc[j * WINDOW:(j + 1) * WINDOW, :] / dens[h]


def _swa_prompt(proj, bsz, t_len, bias_tab, sink_tab):
    n_blk = t_len // WINDOW
    cur = lambda col: (lambda b, n: (b * n_blk + n, col))
    prev = lambda col: (lambda b, n: (b * n_blk + jnp.maximum(n - 1, 0), col))
    return pl.pallas_call(
        _swa_prompt_kernel,
        grid=(bsz, n_blk),
        in_specs=[pl.BlockSpec((WINDOW, SWA_Q_W), cur(0)),
                  pl.BlockSpec((WINDOW, SWA_KV_W), cur(4)), pl.BlockSpec((WINDOW, SWA_KV_W), prev(4)),
                  pl.BlockSpec((WINDOW, SWA_KV_W), cur(5)), pl.BlockSpec((WINDOW, SWA_KV_W), prev(5)),
                  pl.BlockSpec((SWA_HEADS, WINDOW, 2 * WINDOW), lambda b, n: (0, 0, 0)),
                  pl.BlockSpec((SWA_HEADS, LANES), lambda b, n: (0, 0))],
        out_specs=pl.BlockSpec((WINDOW, SWA_Q_W), lambda b, n: (b * n_blk + n, 0)),
        out_shape=jax.ShapeDtypeStruct((bsz * t_len, SWA_Q_W), F32),
        compiler_params=pltpu.CompilerParams(dimension_semantics=("parallel", "arbitrary"),
                                             vmem_limit_bytes=VMEM_LIMIT),
        name="swa_prompt",
    )(proj, proj, proj, proj, proj, bias_tab, sink_tab)


def _swa_sample_kernel(q_ref, kn_ref, vn_ref, kc_ref, vc_ref, bias_ref, bias0_ref, sink_ref, o_ref):
    lane = lax.broadcasted_iota(jnp.int32, (SWA_GROUP, WINDOW), 1)
    rnd = lambda a: a.astype(BF16).astype(F32)
    pairs = [(b, g) for b in range(DEC_TILE) for g in range(SWA_KV_HEADS)]
    heads = lambda g: slice(g * SWA_GROUP, (g + 1) * SWA_GROUP)
    kv_cols = lambda g: slice(g * SWA_HEAD_DIM, (g + 1) * SWA_HEAD_DIM)
    scores = [_mm_nt(q_ref[b, heads(g), :], kc_ref[b, :, kv_cols(g)]) for b, g in pairs]
    probs, own_p, dens = [], [], []
    for (b, g), s in zip(pairs, scores):
        logits = s * (SWA_HEAD_DIM ** -0.5) + bias_ref[heads(g), :]
        logits = jnp.where(lane >= 1, logits, MASKED)
        own = jnp.sum(rnd(q_ref[b, heads(g), :]) * rnd(kn_ref[b:b + 1, kv_cols(g)]), axis=-1,
                      keepdims=True) * (SWA_HEAD_DIM ** -0.5) + bias0_ref[heads(g), 0:1]
        sink = sink_ref[heads(g), 0:1]
        m = jnp.maximum(jnp.maximum(jnp.max(logits, axis=-1, keepdims=True), sink), own)
        p = jnp.exp(logits - m)
        pe = jnp.exp(own - m)
        probs.append(p)
        own_p.append(pe)
        dens.append(jnp.sum(p, axis=-1, keepdims=True) + pe + jnp.exp(sink - m))
    for (b, g), p, pe, den in zip(pairs, probs, own_p, dens):
        acc = _mm(p, vc_ref[b, :, kv_cols(g)]) + rnd(pe) * rnd(vn_ref[b:b + 1, kv_cols(g)])
        o_ref[b, heads(g), :] = acc / den


def _swa_sample(q3, kn, vn, kc, vc, bias_dec, bias0, sink_tab):
    n_dec = q3.shape[0]
    tab = lambda i: (0, 0)
    return pl.pallas_call(
        _swa_sample_kernel,
        grid=(n_dec // DEC_TILE,),
        in_specs=[pl.BlockSpec((DEC_TILE, SWA_HEADS, SWA_HEAD_DIM), lambda i: (i, 0, 0)),
                  pl.BlockSpec((DEC_TILE, SWA_KV_W), lambda i: (i, 0)),
                  pl.BlockSpec((DEC_TILE, SWA_KV_W), lambda i: (i, 0)),
                  pl.BlockSpec((DEC_TILE, WINDOW, SWA_KV_W), lambda i: (i, 0, 0)),
                  pl.BlockSpec((DEC_TILE, WINDOW, SWA_KV_W), lambda i: (i, 0, 0)),
                  pl.BlockSpec((SWA_HEADS, WINDOW), tab), pl.BlockSpec((SWA_HEADS, LANES), tab),
                  pl.BlockSpec((SWA_HEADS, LANES), tab)],
        out_specs=pl.BlockSpec((DEC_TILE, SWA_HEADS, SWA_HEAD_DIM), lambda i: (i, 0, 0)),
        out_shape=jax.ShapeDtypeStruct((n_dec, SWA_HEADS, SWA_HEAD_DIM), F32),
        compiler_params=pltpu.CompilerParams(dimension_semantics=("parallel",),
                                             vmem_limit_bytes=VMEM_LIMIT),
        name="swa_sample",
    )(q3, kn, vn, kc, vc, bias_dec, bias0, sink_tab)


def _t5_bucket(rel):
    exact = REL_BUCKETS // 2
    nf = jnp.maximum(rel, 1).astype(F32)
    large = exact + (jnp.log(nf / exact) / math.log(REL_MAX_DIST / exact)
                     * (REL_BUCKETS - exact)).astype(jnp.int32)
    return jnp.where(rel < exact, rel, jnp.minimum(large, REL_BUCKETS - 1))


def _lane_row(v, width=LANES):
    return jnp.zeros((1, width), F32).at[0, :v.shape[0]].set(v.astype(F32))


def kernel(x_prompt, x_sample, state_gdn, state_gdn_conv, state_lru, state_lru_conv, cache_swa_k, cache_swa_v, w_in_ab, gdn_conv_w, gdn_a_log, gdn_dt_bias, gdn_norm_w, lru_conv_w, lru_conv_b, lru_w_r, lru_b_r, lru_w_i, lru_b_i, lru_lam, w_out_ab, w_in_c, b_in_c, swa_sinks, w_out_c, b_out_c, rel_bias, ln_mix_g, ln_mix_b, ln_ffn_g, ln_ffn_b, peer_w_q, peer_keys, peer_u, peer_v):
    bsz, t_len, _ = x_prompt.shape
    n_dec = x_sample.shape[0]
    n_prompt = bsz * t_len
    assert x_sample.shape[1] == 1 and cache_swa_k.shape[2] == WINDOW

    def peer_ffn(x, layer):
        return _peer_layer(x, peer_w_q[layer].astype(BF16), peer_keys[layer].astype(BF16),
                           peer_u[layer].astype(BF16), _blocked_transpose(peer_v[layer]),
                           ln_ffn_g[layer], ln_ffn_b[layer])

    x = _tokens(x_prompt.reshape(n_prompt, D_MODEL), x_sample.reshape(n_dec, D_MODEL))
    (x1_p, x1_s, p_gdn, p_gdn_conv, p_lru, p_lru_conv, s_gdn, s_gdn_conv, s_lru,
     s_lru_conv) = _layer0_mixers(
        x, bsz, t_len, n_dec, state_gdn, state_gdn_conv, state_lru, state_lru_conv, w_in_ab,
        gdn_conv_w, gdn_a_log, gdn_dt_bias, gdn_norm_w, lru_conv_w, lru_conv_b, lru_w_r, lru_b_r,
        lru_w_i, lru_b_i, lru_lam, w_out_ab, ln_mix_g, ln_mix_b)
    x = peer_ffn(_tokens(x1_p, x1_s), 0)
    x1_p, x1_s, p_k, p_v, s_k, s_v = _layer1_mixers(
        x, bsz, t_len, n_dec, cache_swa_k, cache_swa_v, w_in_c, b_in_c, swa_sinks, w_out_c, b_out_c,
        rel_bias, ln_mix_g, ln_mix_b)
    x = peer_ffn(_tokens(x1_p, x1_s), 1)

    n_real = n_prompt + n_dec
    lead = lambda a: a[None]
    return (x[:n_prompt].reshape(bsz, t_len, D_MODEL), x[n_prompt:n_real].reshape(n_dec, 1, D_MODEL),
            lead(p_gdn), lead(p_gdn_conv), lead(p_lru.reshape(bsz, LRU_WIDTH)), lead(p_lru_conv),
            lead(p_k), lead(p_v),
            lead(s_gdn), lead(s_gdn_conv), lead(s_lru), lead(s_lru_conv), lead(s_k), lead(s_v))


def _blocked_transpose(v):
    n_exp, d = v.shape
    return jnp.swapaxes(v.reshape(n_exp // EXP_BLOCK, EXP_BLOCK, d), 1, 2).astype(BF16)


def _sequence_tails(rows, bsz, t_len, n):
    return jnp.stack([rows[(b + 1) * t_len - n:(b + 1) * t_len] for b in range(bsz)])


def _tokens(xp_rows, xs_rows):
    n_real = xp_rows.shape[0] + xs_rows.shape[0]
    n_tok = -(-n_real // TOK_TILE) * TOK_TILE
    return jnp.concatenate([xp_rows, xs_rows, jnp.zeros((n_tok - n_real, D_MODEL), F32)])


def _layer0_mixers(x, bsz, t_len, n_dec, state_gdn, state_gdn_conv, state_lru, state_lru_conv,
                   w_in_ab, gdn_conv_w, gdn_a_log, gdn_dt_bias, gdn_norm_w, lru_conv_w, lru_conv_b,
                   lru_w_r, lru_b_r, lru_w_i, lru_b_i, lru_lam, w_out_ab, ln_mix_g, ln_mix_b):
    n_prompt = bsz * t_len
    n_real = n_prompt + n_dec
    assert t_len % SEQ_TILE == 0 and n_dec % DEC_TILE == 0 and n_prompt % ROW_TILE == 0
    w_in = w_in_ab[0]
    c0 = GDN_CONV_CH + GDN_V_W
    c1 = c0 + 2 * GDN_HEADS
    w_all = jnp.concatenate([w_in[:, :c0], w_in[:, c1:], w_in[:, c0:c1],
                             jnp.zeros((D_MODEL, LANES - 2 * GDN_HEADS), F32)], axis=1).astype(BF16)
    proj = _project(x, w_all, jnp.zeros((w_all.shape[1],), F32))
    gdn_w = (gdn_conv_w[0], _lane_row(gdn_a_log[0]), _lane_row(gdn_dt_bias[0]),
             gdn_norm_w[0].reshape(1, GDN_DV))
    eye_b = jnp.eye(LRU_BLOCKS, dtype=F32)

    def block_diag(w):
        return (eye_b[:, None, :, None] * w[:, :, None, :]).reshape(LRU_WIDTH, LRU_WIDTH).astype(BF16)

    wide = lambda v: v.reshape(1, LRU_WIDTH)
    lru_w = (lru_conv_w[0], wide(lru_conv_b[0]), block_diag(lru_w_r[0]), wide(lru_b_r[0]),
             block_diag(lru_w_i[0]), wide(lru_b_i[0]), wide(lru_lam[0]))
    oa_p, p_gdn = _gdn_prompt(proj, bsz, t_len, *gdn_w)
    ob_p, p_lru = _lru_prompt(proj, bsz, t_len, lru_w)
    oa_s, ob_s, s_gdn, s_lru = _ab_sample(
        proj, n_prompt, n_dec, jnp.swapaxes(state_gdn_conv[0], 0, 1),
        jnp.swapaxes(state_lru_conv[0], 0, 1), state_gdn[0], state_lru[0], gdn_w, lru_w)
    w_out = w_out_ab[0].astype(BF16)
    zero_b = jnp.zeros((D_MODEL,), F32)
    x1_p = _outproj_ln(x, oa_p, 0, ob_p, 0, w_out, zero_b, ln_mix_g[0], ln_mix_b[0], ROW_TILE)
    x1_s = _outproj_ln(x[n_prompt:n_real], oa_s, 0, ob_s, 0, w_out, zero_b, ln_mix_g[0], ln_mix_b[0],
                       n_dec)
    pre = _sequence_tails(proj, bsz, t_len, CONV_W - 1)
    p_gdn_conv = pre[:, :, :GDN_CONV_CH]
    p_lru_conv = pre[:, :, c0:c0 + LRU_WIDTH]
    new = proj[n_prompt:n_real]
    s_gdn_conv = jnp.concatenate([state_gdn_conv[0][:, 1:], new[:, None, :GDN_CONV_CH]], axis=1)
    s_lru_conv = jnp.concatenate([state_lru_conv[0][:, 1:], new[:, None, c0:c0 + LRU_WIDTH]], axis=1)
    return x1_p, x1_s, p_gdn, p_gdn_conv, p_lru, p_lru_conv, s_gdn, s_gdn_conv, s_lru, s_lru_conv


def _layer1_mixers(x, bsz, t_len, n_dec, cache_swa_k, cache_swa_v, w_in_c, b_in_c, swa_sinks,
                   w_out_c, b_out_c, rel_bias, ln_mix_g, ln_mix_b):
    n_prompt = bsz * t_len
    n_real = n_prompt + n_dec
    assert t_len % WINDOW == 0 and n_dec % DEC_TILE == 0 and n_prompt % ROW_TILE == 0
    proj = _project(x, w_in_c[0].astype(BF16), b_in_c[0])
    rel = jnp.arange(WINDOW)[:, None] + WINDOW - jnp.arange(2 * WINDOW)[None, :]
    bias_vec = rel_bias.astype(F32)[_t5_bucket(jnp.arange(WINDOW))]
    pick = (jnp.clip(rel, 0, WINDOW - 1)[:, :, None] == jnp.arange(WINDOW)).astype(F32)
    bias_tab = jnp.einsum('qkd,dh->hqk', pick, bias_vec, precision=lax.Precision.HIGHEST)
    bias_dec = bias_vec[jnp.clip(WINDOW - jnp.arange(WINDOW), 0, WINDOW - 1)].T
    bias_own = jnp.broadcast_to(bias_vec[0][:, None], (SWA_HEADS, LANES))
    sink_tab = jnp.broadcast_to(swa_sinks[0].astype(F32)[:, None], (SWA_HEADS, LANES))
    attn_p = _swa_prompt(proj, bsz, t_len, bias_tab, sink_tab)
    new = proj[n_prompt:n_real]
    kn, vn = new[:, SWA_Q_W:SWA_Q_W + SWA_KV_W], new[:, SWA_Q_W + SWA_KV_W:]
    kc = cache_swa_k[0].reshape(n_dec, WINDOW, SWA_KV_W)
    vc = cache_swa_v[0].reshape(n_dec, WINDOW, SWA_KV_W)
    attn_s = _swa_sample(new[:, :SWA_Q_W].reshape(n_dec, SWA_HEADS, SWA_HEAD_DIM), kn, vn, kc, vc,
                         bias_dec, bias_own, sink_tab).reshape(n_dec, SWA_Q_W)
    w_out = w_out_c[0].astype(BF16)
    x1_p = _outproj_ln(x, attn_p, 0, attn_p, 1, w_out, b_out_c[0], ln_mix_g[1], ln_mix_b[1], ROW_TILE)
    x1_s = _outproj_ln(x[n_prompt:n_real], attn_s, 0, attn_s, 1, w_out, b_out_c[0], ln_mix_g[1],
                       ln_mix_b[1], n_dec)
    kv_p = _sequence_tails(proj, bsz, t_len, WINDOW)[:, :, SWA_Q_W:]
    heads = (SWA_KV_HEADS, SWA_HEAD_DIM)
    p_k = kv_p[:, :, :SWA_KV_W].reshape(bsz, WINDOW, *heads)
    p_v = kv_p[:, :, SWA_KV_W:].reshape(bsz, WINDOW, *heads)
    s_k = jnp.concatenate([kc[:, 1:], kn[:, None]], axis=1).reshape(n_dec, WINDOW, *heads)
    s_v = jnp.concatenate([vc[:, 1:], vn[:, None]], axis=1).reshape(n_dec, WINDOW, *heads)
    return x1_p, x1_s, p_k, p_v, s_k, s_v
```

```python
import functools
import math

import jax
import jax.numpy as jnp
from jax import lax
from jax.experimental import pallas as pl
from jax.experimental.pallas import tpu as pltpu

F32 = jnp.float32
BF16 = jnp.bfloat16

D_MODEL = 1024
DEPTH = 2
DN_ALPHA = (2 * DEPTH) ** 0.25
LN_EPS = 1e-5

PEER_HEADS = 8
PEER_N_KEYS = 128
PEER_D_SUB = 128
PEER_TOPK = 16
PEER_Q_W = PEER_HEADS * 2 * PEER_D_SUB

LANES = 128
SUBLANES = 8
VMEM_LIMIT = 56 * 1024 * 1024

TOK_TILE = 256
EXP_BLOCK = 2048
MXU_PARTS = 8


def _oddeven_merge(lo, hi, r):
    step = r * 2
    if step < hi - lo:
        yield from _oddeven_merge(lo, hi, step)
        yield from _oddeven_merge(lo + r, hi, step)
        for i in range(lo + r, hi - r, step):
            yield (i, i + r)
    else:
        yield (lo, lo + r)


def _oddeven_sort_pairs(lo, hi):
    if hi - lo >= 1:
        mid = lo + (hi - lo) // 2
        yield from _oddeven_sort_pairs(lo, mid)
        yield from _oddeven_sort_pairs(mid + 1, hi)
        yield from _oddeven_merge(lo, hi, 1)


_SORT16 = tuple(_oddeven_sort_pairs(0, 15))


def _cmpx(v, i, j):
    a, b = v[i], v[j]
    if b is None:
        return
    if a is None:
        v[i], v[j] = b, None
        return
    v[i], v[j] = jnp.maximum(a, b), jnp.minimum(a, b)


def _sort16_desc(v):
    v = list(v)
    for i, j in _SORT16:
        _cmpx(v, i, j)
    return v


def _merge_top16(x, y):
    v = []
    for k in range(16):
        a, b = x[k], y[15 - k]
        v.append(b if a is None else a if b is None else jnp.maximum(a, b))
    for d in (8, 4, 2, 1):
        for i in range(16):
            if not i & d:
                _cmpx(v, i, i + d)
    return v


def _top16_of_keys(s_t):
    v = _sort16_desc([s_t[SUBLANES * r:SUBLANES * (r + 1), :] for r in range(16)])
    for shift in (4, 2, 1):
        v = _merge_top16(v, [pltpu.roll(a, shift, 0) for a in v])
    return v


_PAIR_ROWS = [[(k, l) for l in range(PEER_TOPK // (k + 1))] for k in range(PEER_TOPK)]


def _top16_pair_sums(a, b):
    top = [a[0] + b[l] for l in range(16)]
    rest = [a[k] + b[l] for row in _PAIR_ROWS[1:] for (k, l) in row]
    for lo in range(0, len(rest), 16):
        chunk = rest[lo:lo + 16]
        chunk = chunk + [None] * (16 - len(chunk))
        top = _merge_top16(top, _sort16_desc(chunk))
    return top


def _peer_select_kernel(x_ref, wq_ref, keys_ref, xt_ref, thr_ref, a1_ref, s2_ref, b2_ref,
                        s1_scr, top_scr, stat_scr):
    x = x_ref[...]
    xt_ref[...] = x.T.astype(BF16)
    q = jnp.dot(x.astype(BF16), wq_ref[...], preferred_element_type=F32).astype(BF16)
    nt = (((1,), (1,)), ((), ()))
    sub = lambda hc: slice(hc * PEER_D_SUB, (hc + 1) * PEER_D_SUB)
    for h in range(PEER_HEADS):
        s1_scr[h] = lax.dot_general(keys_ref[h, 0], q[:, sub(2 * h)], nt, preferred_element_type=F32)
        s2_ref[h] = lax.dot_general(keys_ref[h, 1], q[:, sub(2 * h + 1)], nt,
                                    preferred_element_type=F32)
    chunks = [slice(tc * LANES, (tc + 1) * LANES) for tc in range(TOK_TILE // LANES)]
    for h in range(PEER_HEADS):
        for lanes in chunks:
            for c, s_t in ((0, s1_scr[h, :, lanes]), (1, s2_ref[h, :, lanes])):
                for k, v in enumerate(_top16_of_keys(s_t)):
                    top_scr[c, k, h:h + 1, lanes] = v[0:1, :]
    for lanes in chunks:
        a = [top_scr[0, k, :, lanes] for k in range(PEER_TOPK)]
        b = [top_scr[1, k, :, lanes] for k in range(PEER_TOPK)]
        top = _top16_pair_sums(a, b)
        m = a[0] + b[0]
        z = jnp.exp(top[0] - m)
        for k in range(1, PEER_TOPK):
            z = z + jnp.exp(top[k] - m)
        stat_scr[0, :, lanes] = top[PEER_TOPK - 1]
        stat_scr[1, :, lanes] = 1.0 / z
    for h in range(PEER_HEADS):
        row = lambda ref, i: ref[i, h:h + 1, :]
        s1 = s1_scr[h]
        thr_ref[h] = row(stat_scr, 0) - s1
        a1_ref[h] = jnp.exp(s1 - top_scr[0, 0, h:h + 1, :]) * row(stat_scr, 1)
        b2_ref[h] = jnp.exp(s2_ref[h] - top_scr[1, 0, h:h + 1, :])


def _gelu_tanh(x):
    c = math.sqrt(2.0 / math.pi)
    half = 0.5 * x
    return half + half * jnp.tanh(x * (c + (c * 0.044715) * (x * x)))


def _peer_dense_kernel(xt_ref, thr_ref, a1_ref, s2_ref, b2_ref, u_ref, vt_ref, yt_ref, act_scr, w_scr):
    e = pl.program_id(1)
    part_rows = EXP_BLOCK // MXU_PARTS

    @pl.when(e == 0)
    def _():
        yt_ref[...] = jnp.zeros_like(yt_ref)

    def gate_block(r, tc):
        lanes = slice(tc * LANES, (tc + 1) * LANES)
        rows = slice(r * PEER_N_KEYS, (r + 1) * PEER_N_KEYS)
        w = None
        for h in range(PEER_HEADS):
            thr = thr_ref[h, r:r + 1, lanes]
            a1 = a1_ref[h, r:r + 1, lanes]
            term = jnp.where(s2_ref[h, :, lanes] >= thr, b2_ref[h, :, lanes], 0.0) * a1
            w = term if w is None else w + term
        w_scr[rows, lanes] = (w * _gelu_tanh(act_scr[rows, lanes])).astype(BF16)

    for part in range(MXU_PARTS):
        ra = slice(part * part_rows, (part + 1) * part_rows)
        act_scr[ra, :] = jnp.dot(u_ref[ra, :], xt_ref[...], preferred_element_type=F32)
        for r in range(part * part_rows // PEER_N_KEYS, (part + 1) * part_rows // PEER_N_KEYS):
            for tc in range(TOK_TILE // LANES):
                gate_block(r, tc)
        yt_ref[...] += jnp.dot(vt_ref[:, ra], w_scr[ra, :], preferred_element_type=F32)


def _resid_ln_t_kernel(x_ref, yt_ref, g_ref, b_ref, o_ref):
    z = DN_ALPHA * x_ref[...] + yt_ref[...].T
    mu = jnp.mean(z, axis=-1, keepdims=True)
    zc = z - mu
    var = jnp.mean(zc * zc, axis=-1, keepdims=True)
    o_ref[...] = zc * lax.rsqrt(var + LN_EPS) * g_ref[...] + b_ref[...]


def _peer_layer(x, wq, keys, u, vt, ln_g, ln_b):
    n_tok = x.shape[0]
    n_tiles = n_tok // TOK_TILE
    n_exp = u.shape[0]
    fac = jax.ShapeDtypeStruct((PEER_HEADS, PEER_N_KEYS, n_tok), F32)
    fac_spec = pl.BlockSpec((PEER_HEADS, PEER_N_KEYS, TOK_TILE), lambda t: (0, 0, t))
    xt, thr, a1, s2, b2 = pl.pallas_call(
        _peer_select_kernel,
        grid=(n_tiles,),
        in_specs=[
            pl.BlockSpec((TOK_TILE, D_MODEL), lambda t: (t, 0)),
            pl.BlockSpec((D_MODEL, PEER_Q_W), lambda t: (0, 0)),
            pl.BlockSpec((PEER_HEADS, 2, PEER_N_KEYS, PEER_D_SUB), lambda t: (0, 0, 0, 0)),
        ],
        out_specs=[pl.BlockSpec((D_MODEL, TOK_TILE), lambda t: (0, t))] + [fac_spec] * 4,
        out_shape=[jax.ShapeDtypeStruct((D_MODEL, n_tok), BF16)] + [fac] * 4,
        scratch_shapes=[pltpu.VMEM((PEER_HEADS, PEER_N_KEYS, TOK_TILE), F32),
                        pltpu.VMEM((2, PEER_TOPK, PEER_HEADS, TOK_TILE), F32),
                        pltpu.VMEM((2, PEER_HEADS, TOK_TILE), F32)],
        compiler_params=pltpu.CompilerParams(dimension_semantics=("parallel",),
                                             vmem_limit_bytes=VMEM_LIMIT),
        name="peer_select",
    )(x, wq, keys)

    fac_spec2 = pl.BlockSpec((PEER_HEADS, PEER_N_KEYS, TOK_TILE), lambda t, e: (0, 0, t))
    row_spec = pl.BlockSpec((PEER_HEADS, EXP_BLOCK // PEER_N_KEYS, TOK_TILE), lambda t, e: (0, e, t))
    yt = pl.pallas_call(
        _peer_dense_kernel,
        grid=(n_tiles, n_exp // EXP_BLOCK),
        in_specs=[pl.BlockSpec((D_MODEL, TOK_TILE), lambda t, e: (0, t)),
                  row_spec, row_spec, fac_spec2, fac_spec2] + [
            pl.BlockSpec((EXP_BLOCK, D_MODEL), lambda t, e: (e, 0)),
            pl.BlockSpec((D_MODEL, EXP_BLOCK), lambda t, e: (0, e)),
        ],
        out_specs=pl.BlockSpec((D_MODEL, TOK_TILE), lambda t, e: (0, t)),
        out_shape=jax.ShapeDtypeStruct((D_MODEL, n_tok), F32),
        scratch_shapes=[pltpu.VMEM((EXP_BLOCK, TOK_TILE), F32), pltpu.VMEM((EXP_BLOCK, TOK_TILE), BF16)],
        compiler_params=pltpu.CompilerParams(dimension_semantics=("parallel", "arbitrary"),
                                             vmem_limit_bytes=VMEM_LIMIT),
        name="peer_dense",
    )(xt, thr, a1, s2, b2, u, vt)

    return pl.pallas_call(
        _resid_ln_t_kernel,
        grid=(n_tiles,),
        in_specs=[
            pl.BlockSpec((TOK_TILE, D_MODEL), lambda t: (t, 0)),
            pl.BlockSpec((D_MODEL, TOK_TILE), lambda t: (0, t)),
            pl.BlockSpec((1, D_MODEL), lambda t: (0, 0)),
            pl.BlockSpec((1, D_MODEL), lambda t: (0, 0)),
        ],
        out_specs=pl.BlockSpec((TOK_TILE, D_MODEL), lambda t: (t, 0)),
        out_shape=jax.ShapeDtypeStruct((n_tok, D_MODEL), F32),
        compiler_params=pltpu.CompilerParams(dimension_semantics=("parallel",)),
        name="peer_resid_ln",
    )(x, yt, ln_g.reshape(1, D_MODEL), ln_b.reshape(1, D_MODEL))


GDN_HEADS = 4
GDN_DK = 128
GDN_DV = 128
GDN_CHUNK = 64
CONV_W = 4
GDN_QK_W = GDN_HEADS * GDN_DK
GDN_V_W = GDN_HEADS * GDN_DV
GDN_CONV_CH = 2 * GDN_QK_W + GDN_V_W
LRU_WIDTH = 512
LRU_BLOCKS = 8
LRU_C = 8.0
SWA_HEADS = 16
SWA_KV_HEADS = 4
SWA_HEAD_DIM = 64
SWA_GROUP = SWA_HEADS // SWA_KV_HEADS
SWA_Q_W = SWA_HEADS * SWA_HEAD_DIM
SWA_KV_W = SWA_KV_HEADS * SWA_HEAD_DIM
WINDOW = 128
REL_BUCKETS = 32
REL_MAX_DIST = 128
MASKED = -1e30

SEQ_TILE = 256
ROW_TILE = 256
DEC_TILE = 8

_NT = (((1,), (1,)), ((), ()))
_TN = (((0,), (0,)), ((), ()))


def _mm(a, b):
    return jnp.dot(a.astype(BF16), b.astype(BF16), preferred_element_type=F32)


def _mm_nt(a, b):
    return lax.dot_general(a.astype(BF16), b.astype(BF16), _NT, preferred_element_type=F32)


def _mm_tn(a, b):
    return lax.dot_general(a.astype(BF16), b.astype(BF16), _TN, preferred_element_type=F32)


def _mm_split(a, b):
    a_hi = a.astype(BF16)
    b_hi = b.astype(BF16)
    a_lo = (a - a_hi.astype(F32)).astype(BF16)
    b_lo = (b - b_hi.astype(F32)).astype(BF16)
    dot = functools.partial(jnp.dot, preferred_element_type=F32)
    return dot(a_hi, b_hi) + (dot(a_hi, b_lo) + dot(a_lo, b_hi))


def _sigmoid(x):
    return 1.0 / (1.0 + jnp.exp(-x))


def _silu(x):
    return x * _sigmoid(x)


def _softplus(x):
    return jnp.maximum(x, 0.0) + jnp.log1p(jnp.exp(-jnp.abs(x)))


def _layer_norm(z, g, b):
    mu = jnp.mean(z, axis=-1, keepdims=True)
    zc = z - mu
    var = jnp.mean(zc * zc, axis=-1, keepdims=True)
    return zc * lax.rsqrt(var + LN_EPS) * g + b


def _l2_normalize(x):
    return x * lax.rsqrt(jnp.sum(x * x, axis=-1, keepdims=True) + 1e-6)


def _causal_conv(x, xp_scr, w_ref, n_rows):
    xp_scr[8:8 + n_rows, :] = x
    y = x * w_ref[CONV_W - 1:CONV_W, :]
    for i in range(CONV_W - 1):
        y = y + xp_scr[5 + i:5 + i + n_rows, :] * w_ref[i:i + 1, :]
    xp_scr[0:8, :] = x[n_rows - 8:n_rows, :]
    return y


def _gdn_gates(ab, alog_row, dtb_row):
    g = -jnp.exp(alog_row) * _softplus(ab + dtb_row)
    return g, _sigmoid(ab)


def _gdn_out_norm(o, z, nw_row):
    o = o * lax.rsqrt(jnp.mean(o * o, axis=-1, keepdims=True) + 1e-6) * nw_row
    return o * _silu(z)


def _lru_coeffs(xc, wr_ref, br_ref, wi_ref, bi_ref, lam_ref):
    r = _sigmoid(_mm(xc, wr_ref[...]) + br_ref[...])
    i = _sigmoid(_mm(xc, wi_ref[...]) + bi_ref[...])
    log_a = -LRU_C * r * _softplus(-lam_ref[...])
    a = jnp.exp(log_a)
    one_minus_a2 = -jnp.tanh(log_a) * (a * a + 1.0)
    return a, jnp.sqrt(one_minus_a2) * (i * xc)


def _proj_kernel(x_ref, w_ref, b_ref, o_ref):
    o_ref[...] = _mm(x_ref[...], w_ref[...]) + b_ref[...]


def _project(x, w, b):
    n, k = x.shape
    m = w.shape[1]
    return pl.pallas_call(
        _proj_kernel,
        grid=(n // ROW_TILE,),
        in_specs=[pl.BlockSpec((ROW_TILE, k), lambda t: (t, 0)),
                  pl.BlockSpec((k, m), lambda t: (0, 0)),
                  pl.BlockSpec((1, m), lambda t: (0, 0))],
        out_specs=pl.BlockSpec((ROW_TILE, m), lambda t: (t, 0)),
        out_shape=jax.ShapeDtypeStruct((n, m), F32),
        compiler_params=pltpu.CompilerParams(dimension_semantics=("parallel",),
                                             vmem_limit_bytes=VMEM_LIMIT),
        name="proj",
    )(x, w, b.reshape(1, m))


def _outproj_ln_kernel(x_ref, oa_ref, ob_ref, w_ref, b_ref, g_ref, beta_ref, o_ref):
    half = oa_ref.shape[1]
    y = _mm(oa_ref[...], w_ref[0:half, :]) + _mm(ob_ref[...], w_ref[half:2 * half, :]) + b_ref[...]
    o_ref[...] = _layer_norm(DN_ALPHA * x_ref[...] + y, g_ref[...], beta_ref[...])


def _outproj_ln(x, oa, oa_col, ob, ob_col, w, b, g, beta, tile):
    n = oa.shape[0]
    half = w.shape[0] // 2
    row = lambda t: (0, 0)
    return pl.pallas_call(
        _outproj_ln_kernel,
        grid=(n // tile,),
        in_specs=[pl.BlockSpec((tile, D_MODEL), lambda t: (t, 0)),
                  pl.BlockSpec((tile, half), lambda t: (t, oa_col)),
                  pl.BlockSpec((tile, half), lambda t: (t, ob_col)),
                  pl.BlockSpec((2 * half, D_MODEL), row),
                  pl.BlockSpec((1, D_MODEL), row), pl.BlockSpec((1, D_MODEL), row),
                  pl.BlockSpec((1, D_MODEL), row)],
        out_specs=pl.BlockSpec((tile, D_MODEL), lambda t: (t, 0)),
        out_shape=jax.ShapeDtypeStruct((n, D_MODEL), F32),
        compiler_params=pltpu.CompilerParams(dimension_semantics=("parallel",),
                                             vmem_limit_bytes=VMEM_LIMIT),
        name="outproj_ln",
    )(x, oa, ob, w, b.reshape(1, D_MODEL), g.reshape(1, D_MODEL), beta.reshape(1, D_MODEL))


def _gdn_prompt_kernel(qkv_ref, ab_ref, z_ref, cw_ref, alog_ref, dtb_ref, nw_ref, o_ref, s_out_ref,
                       xp_scr, s_scr, q_scr, k_scr, v_scr, gc_scr, beta_scr, u_scr, w_scr, qk_scr):
    t = pl.program_id(1)

    @pl.when(t == 0)
    def _():
        xp_scr[0:8, :] = jnp.zeros((8, GDN_CONV_CH), F32)
        s_scr[...] = jnp.zeros_like(s_scr)

    y = _silu(_causal_conv(qkv_ref[...], xp_scr, cw_ref, SEQ_TILE))
    for h in range(GDN_HEADS):
        cols = slice(h * GDN_DK, (h + 1) * GDN_DK)
        q_scr[:, cols] = _l2_normalize(y[:, cols]) * (GDN_DK ** -0.5)
        k_scr[:, cols] = _l2_normalize(y[:, GDN_QK_W + h * GDN_DK:GDN_QK_W + (h + 1) * GDN_DK])
    v_scr[...] = y[:, 2 * GDN_QK_W:]
    g, beta = _gdn_gates(ab_ref[...], alog_ref[...], dtb_ref[...])
    beta_scr[...] = beta
    pos = lax.broadcasted_iota(jnp.int32, g.shape, 0) % GDN_CHUNK
    shift = 1
    while shift < GDN_CHUNK:
        g = g + jnp.where(pos >= shift, pltpu.roll(g, shift, 0), 0.0)
        shift *= 2
    gc_scr[...] = g

    ri = lax.broadcasted_iota(jnp.int32, (GDN_CHUNK, GDN_CHUNK), 0)
    ci = lax.broadcasted_iota(jnp.int32, (GDN_CHUNK, GDN_CHUNK), 1)
    eye = ri == ci
    incl = ri >= ci
    strict = ri > ci

    n_chunks = SEQ_TILE // GDN_CHUNK
    pairs = [(c, h) for c in range(n_chunks) for h in range(GDN_HEADS)]
    rows_of = lambda c: slice(c * GDN_CHUNK, (c + 1) * GDN_CHUNK)
    cols_of = lambda h: slice(h * GDN_DK, (h + 1) * GDN_DK)
    gcol_of = lambda c, h: gc_scr[rows_of(c), h:h + 1]

    pw, inv = [], []
    for c, h in pairs:
        k = k_scr[rows_of(c), cols_of(h)]
        gcol = gcol_of(c, h)
        bcol = beta_scr[rows_of(c), GDN_HEADS + h:GDN_HEADS + h + 1]
        grow = jnp.sum(jnp.where(eye, gcol, 0.0), axis=0, keepdims=True)
        decay = jnp.where(incl, jnp.exp(jnp.where(incl, gcol - grow, 0.0)), 0.0)
        kb = k * bcol
        neg = -jnp.where(strict, _mm_nt(kb, k) * decay, 0.0)
        qk_scr[rows_of(c), h * GDN_CHUNK:(h + 1) * GDN_CHUNK] = jnp.where(
            incl, _mm_nt(q_scr[rows_of(c), cols_of(h)], k) * decay, 0.0)
        pw.append(neg)
        inv.append(jnp.where(eye, 1.0, 0.0) + neg)
    for _ in range(5):
        pw = [_mm_split(p, p) for p in pw]
        inv = [a + _mm_split(a, p) for a, p in zip(inv, pw)]
    for (c, h), a in zip(pairs, inv):
        bcol = beta_scr[rows_of(c), GDN_HEADS + h:GDN_HEADS + h + 1]
        kb = k_scr[rows_of(c), cols_of(h)] * bcol
        u_scr[rows_of(c), cols_of(h)] = _mm_split(a, v_scr[rows_of(c), cols_of(h)] * bcol)
        w_scr[rows_of(c), cols_of(h)] = _mm_split(a, kb * jnp.exp(gcol_of(c, h)))

    state = [s_scr[h] for h in range(GDN_HEADS)]
    for c in range(n_chunks):
        rows = rows_of(c)
        v_new = [u_scr[rows, cols_of(h)] - _mm(w_scr[rows, cols_of(h)], state[h])
                 for h in range(GDN_HEADS)]
        for h in range(GDN_HEADS):
            cols = cols_of(h)
            gcol = gcol_of(c, h)
            o = (_mm(q_scr[rows, cols] * jnp.exp(gcol), state[h])
                 + _mm(qk_scr[rows, h * GDN_CHUNK:(h + 1) * GDN_CHUNK], v_new[h]))
            g_last = gcol[GDN_CHUNK - 1:GDN_CHUNK, :]
            state[h] = (state[h] * jnp.exp(g_last)
                        + _mm_tn(k_scr[rows, cols] * jnp.exp(g_last - gcol), v_new[h]))
            o_ref[rows, cols] = _gdn_out_norm(o, z_ref[rows, cols], nw_ref[...])
    for h in range(GDN_HEADS):
        s_scr[h] = state[h]

    @pl.when(t == pl.num_programs(1) - 1)
    def _():
        s_out_ref[0] = s_scr[...]


def _gdn_prompt(proj, bsz, t_len, cw, alog_row, dtb_row, nw_row):
    n_t = t_len // SEQ_TILE
    row = lambda b, t: (0, 0)
    return pl.pallas_call(
        _gdn_prompt_kernel,
        grid=(bsz, n_t),
        in_specs=[pl.BlockSpec((SEQ_TILE, GDN_CONV_CH), lambda b, t: (b * n_t + t, 0)),
                  pl.BlockSpec((SEQ_TILE, LANES), lambda b, t: (b * n_t + t, 24)),
                  pl.BlockSpec((SEQ_TILE, GDN_V_W), lambda b, t: (b * n_t + t, 3)),
                  pl.BlockSpec((CONV_W, GDN_CONV_CH), row),
                  pl.BlockSpec((1, LANES), row), pl.BlockSpec((1, LANES), row),
                  pl.BlockSpec((1, GDN_DV), row)],
        out_specs=[pl.BlockSpec((SEQ_TILE, GDN_V_W), lambda b, t: (b * n_t + t, 0)),
                   pl.BlockSpec((1, GDN_HEADS, GDN_DK, GDN_DV), lambda b, t: (b, 0, 0, 0))],
        out_shape=[jax.ShapeDtypeStruct((bsz * t_len, GDN_V_W), F32),
                   jax.ShapeDtypeStruct((bsz, GDN_HEADS, GDN_DK, GDN_DV), F32)],
        scratch_shapes=[pltpu.VMEM((SEQ_TILE + 8, GDN_CONV_CH), F32),
                        pltpu.VMEM((GDN_HEADS, GDN_DK, GDN_DV), F32),
                        pltpu.VMEM((SEQ_TILE, GDN_QK_W), F32), pltpu.VMEM((SEQ_TILE, GDN_QK_W), F32),
                        pltpu.VMEM((SEQ_TILE, GDN_V_W), F32),
                        pltpu.VMEM((SEQ_TILE, LANES), F32), pltpu.VMEM((SEQ_TILE, LANES), F32),
                        pltpu.VMEM((SEQ_TILE, GDN_V_W), F32), pltpu.VMEM((SEQ_TILE, GDN_QK_W), F32),
                        pltpu.VMEM((SEQ_TILE, GDN_HEADS * GDN_CHUNK), F32)],
        compiler_params=pltpu.CompilerParams(dimension_semantics=("parallel", "arbitrary"),
                                             vmem_limit_bytes=VMEM_LIMIT),
        name="gdn_prompt",
    )(proj, proj, proj, cw, alog_row, dtb_row, nw_row)


def _lru_prompt_kernel(xr_ref, gate_ref, cw_ref, cb_ref, wr_ref, br_ref, wi_ref, bi_ref, lam_ref,
                       o_ref, h_out_ref, xp_scr, h_scr):
    t = pl.program_id(1)

    @pl.when(t == 0)
    def _():
        xp_scr[0:8, :] = jnp.zeros((8, LRU_WIDTH), F32)
        h_scr[...] = jnp.zeros_like(h_scr)

    xc = _causal_conv(xr_ref[...], xp_scr, cw_ref, SEQ_TILE) + cb_ref[...]
    a, b = _lru_coeffs(xc, wr_ref, br_ref, wi_ref, bi_ref, lam_ref)
    pos = lax.broadcasted_iota(jnp.int32, a.shape, 0)
    shift = 1
    while shift < SEQ_TILE:
        valid = pos >= shift
        b = jnp.where(valid, a * pltpu.roll(b, shift, 0) + b, b)
        a = jnp.where(valid, a * pltpu.roll(a, shift, 0), a)
        shift *= 2
    h = b + a * h_scr[...]
    h_scr[...] = h[SEQ_TILE - 1:SEQ_TILE, :]
    o_ref[...] = jax.nn.gelu(gate_ref[...]) * h

    @pl.when(t == pl.num_programs(1) - 1)
    def _():
        h_out_ref[0] = h[SEQ_TILE - 1:SEQ_TILE, :]


def _lru_prompt(proj, bsz, t_len, lru_w):
    n_t = t_len // SEQ_TILE
    row = lambda b, t: (0, 0)
    wide = pl.BlockSpec((1, LRU_WIDTH), row)
    sq = pl.BlockSpec((LRU_WIDTH, LRU_WIDTH), row)
    return pl.pallas_call(
        _lru_prompt_kernel,
        grid=(bsz, n_t),
        in_specs=[pl.BlockSpec((SEQ_TILE, LRU_WIDTH), lambda b, t: (b * n_t + t, 4)),
                  pl.BlockSpec((SEQ_TILE, LRU_WIDTH), lambda b, t: (b * n_t + t, 5)),
                  pl.BlockSpec((CONV_W, LRU_WIDTH), row), wide, sq, wide, sq, wide, wide],
        out_specs=[pl.BlockSpec((SEQ_TILE, LRU_WIDTH), lambda b, t: (b * n_t + t, 0)),
                   pl.BlockSpec((1, 1, LRU_WIDTH), lambda b, t: (b, 0, 0))],
        out_shape=[jax.ShapeDtypeStruct((bsz * t_len, LRU_WIDTH), F32),
                   jax.ShapeDtypeStruct((bsz, 1, LRU_WIDTH), F32)],
        scratch_shapes=[pltpu.VMEM((SEQ_TILE + 8, LRU_WIDTH), F32), pltpu.VMEM((1, LRU_WIDTH), F32)],
        compiler_params=pltpu.CompilerParams(dimension_semantics=("parallel", "arbitrary"),
                                             vmem_limit_bytes=VMEM_LIMIT),
        name="lru_prompt",
    )(proj, proj, *lru_w)


def _ab_sample_kernel(qkv_ref, ab_ref, z_ref, xr_ref, gate_ref, gbuf_ref, lbuf_ref, s_ref, h0_ref,
                      cw_ref, alog_ref, dtb_ref, nw_ref,
                      lcw_ref, lcb_ref, wr_ref, br_ref, wi_ref, bi_ref, lam_ref,
                      oa_ref, ob_ref, s_out_ref, h_out_ref, o_scr):
    y = qkv_ref[...] * cw_ref[CONV_W - 1:CONV_W, :]
    for i in range(CONV_W - 1):
        y = y + gbuf_ref[i] * cw_ref[i:i + 1, :]
    y = _silu(y)
    g, beta = _gdn_gates(ab_ref[...], alog_ref[...], dtb_ref[...])
    eg = jnp.exp(g)
    for h in range(GDN_HEADS):
        cols = slice(h * GDN_DK, (h + 1) * GDN_DK)
        q = _l2_normalize(y[:, cols]) * (GDN_DK ** -0.5)
        k = _l2_normalize(y[:, GDN_QK_W + h * GDN_DK:GDN_QK_W + (h + 1) * GDN_DK])
        v = y[:, 2 * GDN_QK_W + h * GDN_DV:2 * GDN_QK_W + (h + 1) * GDN_DV]
        qk = jnp.sum(q * k, axis=-1, keepdims=True)
        q_t = q.T
        k_t = k.T
        for b in range(DEC_TILE):
            s = s_ref[b, h]
            kcol = k_t[:, b:b + 1]
            e = eg[b:b + 1, h:h + 1]
            ks = jnp.sum(s * kcol, axis=0, keepdims=True)
            qs = jnp.sum(s * q_t[:, b:b + 1], axis=0, keepdims=True)
            v_new = beta[b:b + 1, GDN_HEADS + h:GDN_HEADS + h + 1] * (v[b:b + 1, :] - e * ks)
            o_scr[b:b + 1, cols] = e * qs + qk[b:b + 1, :] * v_new
            s_out_ref[b, h] = e * s + kcol * v_new
    for h in range(GDN_HEADS):
        cols = slice(h * GDN_DV, (h + 1) * GDN_DV)
        oa_ref[:, cols] = _gdn_out_norm(o_scr[:, cols], z_ref[:, cols], nw_ref[...])
    xr = xr_ref[...]
    xc = xr * lcw_ref[CONV_W - 1:CONV_W, :] + lcb_ref[...]
    for i in range(CONV_W - 1):
        xc = xc + lbuf_ref[i] * lcw_ref[i:i + 1, :]
    a, bb = _lru_coeffs(xc, wr_ref, br_ref, wi_ref, bi_ref, lam_ref)
    hid = a * h0_ref[...] + bb
    h_out_ref[...] = hid
    ob_ref[...] = jax.nn.gelu(gate_ref[...]) * hid


def _ab_sample(proj, row0, n_dec, gbuf_t, lbuf_t, s0, h0, gdn_w, lru_w):
    blk0 = row0 // DEC_TILE
    row = lambda i: (0, 0)
    wide = pl.BlockSpec((1, LRU_WIDTH), row)
    sq = pl.BlockSpec((LRU_WIDTH, LRU_WIDTH), row)
    return pl.pallas_call(
        _ab_sample_kernel,
        grid=(n_dec // DEC_TILE,),
        in_specs=[pl.BlockSpec((DEC_TILE, GDN_CONV_CH), lambda i: (blk0 + i, 0)),
                  pl.BlockSpec((DEC_TILE, LANES), lambda i: (blk0 + i, 24)),
                  pl.BlockSpec((DEC_TILE, GDN_V_W), lambda i: (blk0 + i, 3)),
                  pl.BlockSpec((DEC_TILE, LRU_WIDTH), lambda i: (blk0 + i, 4)),
                  pl.BlockSpec((DEC_TILE, LRU_WIDTH), lambda i: (blk0 + i, 5)),
                  pl.BlockSpec((CONV_W - 1, DEC_TILE, GDN_CONV_CH), lambda i: (0, i, 0)),
                  pl.BlockSpec((CONV_W - 1, DEC_TILE, LRU_WIDTH), lambda i: (0, i, 0)),
                  pl.BlockSpec((DEC_TILE, GDN_HEADS, GDN_DK, GDN_DV), lambda i: (i, 0, 0, 0)),
                  pl.BlockSpec((DEC_TILE, LRU_WIDTH), lambda i: (i, 0)),
                  pl.BlockSpec((CONV_W, GDN_CONV_CH), row),
                  pl.BlockSpec((1, LANES), row), pl.BlockSpec((1, LANES), row),
                  pl.BlockSpec((1, GDN_DV), row),
                  pl.BlockSpec((CONV_W, LRU_WIDTH), row), wide, sq, wide, sq, wide, wide],
        out_specs=[pl.BlockSpec((DEC_TILE, GDN_V_W), lambda i: (i, 0)),
                   pl.BlockSpec((DEC_TILE, LRU_WIDTH), lambda i: (i, 0)),
                   pl.BlockSpec((DEC_TILE, GDN_HEADS, GDN_DK, GDN_DV), lambda i: (i, 0, 0, 0)),
                   pl.BlockSpec((DEC_TILE, LRU_WIDTH), lambda i: (i, 0))],
        out_shape=[jax.ShapeDtypeStruct((n_dec, GDN_V_W), F32),
                   jax.ShapeDtypeStruct((n_dec, LRU_WIDTH), F32),
                   jax.ShapeDtypeStruct((n_dec, GDN_HEADS, GDN_DK, GDN_DV), F32),
                   jax.ShapeDtypeStruct((n_dec, LRU_WIDTH), F32)],
        scratch_shapes=[pltpu.VMEM((DEC_TILE, GDN_V_W), F32)],
        compiler_params=pltpu.CompilerParams(dimension_semantics=("parallel",),
                                             vmem_limit_bytes=VMEM_LIMIT),
        name="ab_sample",
    )(proj, proj, proj, proj, proj, gbuf_t, lbuf_t, s0, h0, *gdn_w, *lru_w)


def _swa_prompt_kernel(q_ref, kc_ref, kp_ref, vc_ref, vp_ref, bias_ref, sink_ref, o_ref):
    n = pl.program_id(1)
    kcat = jnp.concatenate([kp_ref[...], kc_ref[...]], axis=0)
    vcat = jnp.concatenate([vp_ref[...], vc_ref[...]], axis=0)
    qi = lax.broadcasted_iota(jnp.int32, (WINDOW, 2 * WINDOW), 0)
    ki = lax.broadcasted_iota(jnp.int32, (WINDOW, 2 * WINDOW), 1)
    rel = qi + WINDOW - ki
    mask = (rel >= 0) & (rel < WINDOW) & ((ki >= WINDOW) | (n > 0))
    head_cols = lambda h: slice(h * SWA_HEAD_DIM, (h + 1) * SWA_HEAD_DIM)
    scores = [_mm_nt(jnp.concatenate([q_ref[:, head_cols(g * SWA_GROUP + j)]
                                      for j in range(SWA_GROUP)], axis=0), kcat[:, head_cols(g)])
              for g in range(SWA_KV_HEADS)]
    probs, dens = [], []
    for g in range(SWA_KV_HEADS):
        p_rows = []
        for j in range(SWA_GROUP):
            h = g * SWA_GROUP + j
            logits = scores[g][j * WINDOW:(j + 1) * WINDOW, :] * (SWA_HEAD_DIM ** -0.5) + bias_ref[h]
            logits = jnp.where(mask, logits, MASKED)
            sink = sink_ref[h:h + 1, 0:1]
            m = jnp.maximum(jnp.max(logits, axis=-1, keepdims=True), sink)
            p = jnp.exp(logits - m)
            dens.append(jnp.sum(p, axis=-1, keepdims=True) + jnp.exp(sink - m))
            p_rows.append(p.astype(BF16))
        probs.append(jnp.concatenate(p_rows, axis=0))
    for g in range(SWA_KV_HEADS):
        acc = _mm(probs[g], vcat[:, head_cols(g)])
        for j in range(SWA_GROUP):
            h = g * SWA_GROUP + j
            o_ref[:, head_cols(h)] = acc[j * WINDOW:(j + 1) * WINDOW, :] / dens[h]


def _swa_prompt(proj, bsz, t_len, bias_tab, sink_tab):
    n_blk = t_len // WINDOW
    cur = lambda col: (lambda b, n: (b * n_blk + n, col))
    prev = lambda col: (lambda b, n: (b * n_blk + jnp.maximum(n - 1, 0), col))
    return pl.pallas_call(
        _swa_prompt_kernel,
        grid=(bsz, n_blk),
        in_specs=[pl.BlockSpec((WINDOW, SWA_Q_W), cur(0)),
                  pl.BlockSpec((WINDOW, SWA_KV_W), cur(4)), pl.BlockSpec((WINDOW, SWA_KV_W), prev(4)),
                  pl.BlockSpec((WINDOW, SWA_KV_W), cur(5)), pl.BlockSpec((WINDOW, SWA_KV_W), prev(5)),
                  pl.BlockSpec((SWA_HEADS, WINDOW, 2 * WINDOW), lambda b, n: (0, 0, 0)),
                  pl.BlockSpec((SWA_HEADS, LANES), lambda b, n: (0, 0))],
        out_specs=pl.BlockSpec((WINDOW, SWA_Q_W), lambda b, n: (b * n_blk + n, 0)),
        out_shape=jax.ShapeDtypeStruct((bsz * t_len, SWA_Q_W), F32),
        compiler_params=pltpu.CompilerParams(dimension_semantics=("parallel", "arbitrary"),
                                             vmem_limit_bytes=VMEM_LIMIT),
        name="swa_prompt",
    )(proj, proj, proj, proj, proj, bias_tab, sink_tab)


def _swa_sample_kernel(q_ref, kn_ref, vn_ref, kc_ref, vc_ref, bias_ref, bias0_ref, sink_ref, o_ref):
    lane = lax.broadcasted_iota(jnp.int32, (SWA_GROUP, WINDOW), 1)
    rnd = lambda a: a.astype(BF16).astype(F32)
    pairs = [(b, g) for b in range(DEC_TILE) for g in range(SWA_KV_HEADS)]
    heads = lambda g: slice(g * SWA_GROUP, (g + 1) * SWA_GROUP)
    kv_cols = lambda g: slice(g * SWA_HEAD_DIM, (g + 1) * SWA_HEAD_DIM)
    scores = [_mm_nt(q_ref[b, heads(g), :], kc_ref[b, :, kv_cols(g)]) for b, g in pairs]
    probs, own_p, dens = [], [], []
    for (b, g), s in zip(pairs, scores):
        logits = s * (SWA_HEAD_DIM ** -0.5) + bias_ref[heads(g), :]
        logits = jnp.where(lane >= 1, logits, MASKED)
        own = jnp.sum(rnd(q_ref[b, heads(g), :]) * rnd(kn_ref[b:b + 1, kv_cols(g)]), axis=-1,
                      keepdims=True) * (SWA_HEAD_DIM ** -0.5) + bias0_ref[heads(g), 0:1]
        sink = sink_ref[heads(g), 0:1]
        m = jnp.maximum(jnp.maximum(jnp.max(logits, axis=-1, keepdims=True), sink), own)
        p = jnp.exp(logits - m)
        pe = jnp.exp(own - m)
        probs.append(p)
        own_p.append(pe)
        dens.append(jnp.sum(p, axis=-1, keepdims=True) + pe + jnp.exp(sink - m))
    for (b, g), p, pe, den in zip(pairs, probs, own_p, dens):
        acc = _mm(p, vc_ref[b, :, kv_cols(g)]) + rnd(pe) * rnd(vn_ref[b:b + 1, kv_cols(g)])
        o_ref[b, heads(g), :] = acc / den


def _swa_sample(q3, kn, vn, kc, vc, bias_dec, bias0, sink_tab):
    n_dec = q3.shape[0]
    tab = lambda i: (0, 0)
    return pl.pallas_call(
        _swa_sample_kernel,
        grid=(n_dec // DEC_TILE,),
        in_specs=[pl.BlockSpec((DEC_TILE, SWA_HEADS, SWA_HEAD_DIM), lambda i: (i, 0, 0)),
                  pl.BlockSpec((DEC_TILE, SWA_KV_W), lambda i: (i, 0)),
                  pl.BlockSpec((DEC_TILE, SWA_KV_W), lambda i: (i, 0)),
                  pl.BlockSpec((DEC_TILE, WINDOW, SWA_KV_W), lambda i: (i, 0, 0)),
                  pl.BlockSpec((DEC_TILE, WINDOW, SWA_KV_W), lambda i: (i, 0, 0)),
                  pl.BlockSpec((SWA_HEADS, WINDOW), tab), pl.BlockSpec((SWA_HEADS, LANES), tab),
                  pl.BlockSpec((SWA_HEADS, LANES), tab)],
        out_specs=pl.BlockSpec((DEC_TILE, SWA_HEADS, SWA_HEAD_DIM), lambda i: (i, 0, 0)),
        out_shape=jax.ShapeDtypeStruct((n_dec, SWA_HEADS, SWA_HEAD_DIM), F32),
        compiler_params=pltpu.CompilerParams(dimension_semantics=("parallel",),
                                             vmem_limit_bytes=VMEM_LIMIT),
        name="swa_sample",
    )(q3, kn, vn, kc, vc, bias_dec, bias0, sink_tab)


def _t5_bucket(rel):
    exact = REL_BUCKETS // 2
    nf = jnp.maximum(rel, 1).astype(F32)
    large = exact + (jnp.log(nf / exact) / math.log(REL_MAX_DIST / exact)
                     * (REL_BUCKETS - exact)).astype(jnp.int32)
    return jnp.where(rel < exact, rel, jnp.minimum(large, REL_BUCKETS - 1))


def _lane_row(v, width=LANES):
    return jnp.zeros((1, width), F32).at[0, :v.shape[0]].set(v.astype(F32))


def kernel(x_prompt, x_sample, state_gdn, state_gdn_conv, state_lru, state_lru_conv, cache_swa_k, cache_swa_v, w_in_ab, gdn_conv_w, gdn_a_log, gdn_dt_bias, gdn_norm_w, lru_conv_w, lru_conv_b, lru_w_r, lru_b_r, lru_w_i, lru_b_i, lru_lam, w_out_ab, w_in_c, b_in_c, swa_sinks, w_out_c, b_out_c, rel_bias, ln_mix_g, ln_mix_b, ln_ffn_g, ln_ffn_b, peer_w_q, peer_keys, peer_u, peer_v):
    bsz, t_len, _ = x_prompt.shape
    n_dec = x_sample.shape[0]
    n_prompt = bsz * t_len
    assert x_sample.shape[1] == 1 and cache_swa_k.shape[2] == WINDOW

    def peer_ffn(x, layer):
        return _peer_layer(x, peer_w_q[layer].astype(BF16), peer_keys[layer].astype(BF16),
                           peer_u[layer].astype(BF16), peer_v[layer].T.astype(BF16),
                           ln_ffn_g[layer], ln_ffn_b[layer])

    x = _tokens(x_prompt.reshape(n_prompt, D_MODEL), x_sample.reshape(n_dec, D_MODEL))
    (x1_p, x1_s, p_gdn, p_gdn_conv, p_lru, p_lru_conv, s_gdn, s_gdn_conv, s_lru,
     s_lru_conv) = _layer0_mixers(
        x, bsz, t_len, n_dec, state_gdn, state_gdn_conv, state_lru, state_lru_conv, w_in_ab,
        gdn_conv_w, gdn_a_log, gdn_dt_bias, gdn_norm_w, lru_conv_w, lru_conv_b, lru_w_r, lru_b_r,
        lru_w_i, lru_b_i, lru_lam, w_out_ab, ln_mix_g, ln_mix_b)
    x = peer_ffn(_tokens(x1_p, x1_s), 0)
    x1_p, x1_s, p_k, p_v, s_k, s_v = _layer1_mixers(
        x, bsz, t_len, n_dec, cache_swa_k, cache_swa_v, w_in_c, b_in_c, swa_sinks, w_out_c, b_out_c,
        rel_bias, ln_mix_g, ln_mix_b)
    x = peer_ffn(_tokens(x1_p, x1_s), 1)

    n_real = n_prompt + n_dec
    lead = lambda a: a[None]
    return (x[:n_prompt].reshape(bsz, t_len, D_MODEL), x[n_prompt:n_real].reshape(n_dec, 1, D_MODEL),
            lead(p_gdn), lead(p_gdn_conv), lead(p_lru.reshape(bsz, LRU_WIDTH)), lead(p_lru_conv),
            lead(p_k), lead(p_v),
            lead(s_gdn), lead(s_gdn_conv), lead(s_lru), lead(s_lru_conv), lead(s_k), lead(s_v))


def _sequence_tails(rows, bsz, t_len, n):
    return jnp.stack([rows[(b + 1) * t_len - n:(b + 1) * t_len] for b in range(bsz)])


def _tokens(xp_rows, xs_rows):
    n_real = xp_rows.shape[0] + xs_rows.shape[0]
    n_tok = -(-n_real // TOK_TILE) * TOK_TILE
    return jnp.concatenate([xp_rows, xs_rows, jnp.zeros((n_tok - n_real, D_MODEL), F32)])


def _layer0_mixers(x, bsz, t_len, n_dec, state_gdn, state_gdn_conv, state_lru, state_lru_conv,
                   w_in_ab, gdn_conv_w, gdn_a_log, gdn_dt_bias, gdn_norm_w, lru_conv_w, lru_conv_b,
                   lru_w_r, lru_b_r, lru_w_i, lru_b_i, lru_lam, w_out_ab, ln_mix_g, ln_mix_b):
    n_prompt = bsz * t_len
    n_real = n_prompt + n_dec
    assert t_len % SEQ_TILE == 0 and n_dec % DEC_TILE == 0 and n_prompt % ROW_TILE == 0
    w_in = w_in_ab[0]
    c0 = GDN_CONV_CH + GDN_V_W
    c1 = c0 + 2 * GDN_HEADS
    w_all = jnp.concatenate([w_in[:, :c0], w_in[:, c1:], w_in[:, c0:c1],
                             jnp.zeros((D_MODEL, LANES - 2 * GDN_HEADS), F32)], axis=1).astype(BF16)
    proj = _project(x, w_all, jnp.zeros((w_all.shape[1],), F32))
    gdn_w = (gdn_conv_w[0], _lane_row(gdn_a_log[0]), _lane_row(gdn_dt_bias[0]),
             gdn_norm_w[0].reshape(1, GDN_DV))
    eye_b = jnp.eye(LRU_BLOCKS, dtype=F32)

    def block_diag(w):
        return (eye_b[:, None, :, None] * w[:, :, None, :]).reshape(LRU_WIDTH, LRU_WIDTH).astype(BF16)

    wide = lambda v: v.reshape(1, LRU_WIDTH)
    lru_w = (lru_conv_w[0], wide(lru_conv_b[0]), block_diag(lru_w_r[0]), wide(lru_b_r[0]),
             block_diag(lru_w_i[0]), wide(lru_b_i[0]), wide(lru_lam[0]))
    oa_p, p_gdn = _gdn_prompt(proj, bsz, t_len, *gdn_w)
    ob_p, p_lru = _lru_prompt(proj, bsz, t_len, lru_w)
    oa_s, ob_s, s_gdn, s_lru = _ab_sample(
        proj, n_prompt, n_dec, jnp.swapaxes(state_gdn_conv[0], 0, 1),
        jnp.swapaxes(state_lru_conv[0], 0, 1), state_gdn[0], state_lru[0], gdn_w, lru_w)
    w_out = w_out_ab[0].astype(BF16)
    zero_b = jnp.zeros((D_MODEL,), F32)
    x1_p = _outproj_ln(x, oa_p, 0, ob_p, 0, w_out, zero_b, ln_mix_g[0], ln_mix_b[0], ROW_TILE)
    x1_s = _outproj_ln(x[n_prompt:n_real], oa_s, 0, ob_s, 0, w_out, zero_b, ln_mix_g[0], ln_mix_b[0],
                       n_dec)
    pre = _sequence_tails(proj, bsz, t_len, CONV_W - 1)
    p_gdn_conv = pre[:, :, :GDN_CONV_CH]
    p_lru_conv = pre[:, :, c0:c0 + LRU_WIDTH]
    new = proj[n_prompt:n_real]
    s_gdn_conv = jnp.concatenate([state_gdn_conv[0][:, 1:], new[:, None, :GDN_CONV_CH]], axis=1)
    s_lru_conv = jnp.concatenate([state_lru_conv[0][:, 1:], new[:, None, c0:c0 + LRU_WIDTH]], axis=1)
    return x1_p, x1_s, p_gdn, p_gdn_conv, p_lru, p_lru_conv, s_gdn, s_gdn_conv, s_lru, s_lru_conv


def _layer1_mixers(x, bsz, t_len, n_dec, cache_swa_k, cache_swa_v, w_in_c, b_in_c, swa_sinks,
                   w_out_c, b_out_c, rel_bias, ln_mix_g, ln_mix_b):
    n_prompt = bsz * t_len
    n_real = n_prompt + n_dec
    assert t_len % WINDOW == 0 and n_dec % DEC_TILE == 0 and n_prompt % ROW_TILE == 0
    proj = _project(x, w_in_c[0].astype(BF16), b_in_c[0])
    rel = jnp.arange(WINDOW)[:, None] + WINDOW - jnp.arange(2 * WINDOW)[None, :]
    bias_vec = rel_bias.astype(F32)[_t5_bucket(jnp.arange(WINDOW))]
    pick = (jnp.clip(rel, 0, WINDOW - 1)[:, :, None] == jnp.arange(WINDOW)).astype(F32)
    bias_tab = jnp.einsum('qkd,dh->hqk', pick, bias_vec, precision=lax.Precision.HIGHEST)
    bias_dec = bias_vec[jnp.clip(WINDOW - jnp.arange(WINDOW), 0, WINDOW - 1)].T
    bias_own = jnp.broadcast_to(bias_vec[0][:, None], (SWA_HEADS, LANES))
    sink_tab = jnp.broadcast_to(swa_sinks[0].astype(F32)[:, None], (SWA_HEADS, LANES))
    attn_p = _swa_prompt(proj, bsz, t_len, bias_tab, sink_tab)
    new = proj[n_prompt:n_real]
    kn, vn = new[:, SWA_Q_W:SWA_Q_W + SWA_KV_W], new[:, SWA_Q_W + SWA_KV_W:]
    kc = cache_swa_k[0].reshape(n_dec, WINDOW, SWA_KV_W)
    vc = cache_swa_v[0].reshape(n_dec, WINDOW, SWA_KV_W)
    attn_s = _swa_sample(new[:, :SWA_Q_W].reshape(n_dec, SWA_HEADS, SWA_HEAD_DIM), kn, vn, kc, vc,
                         bias_dec, bias_own, sink_tab).reshape(n_dec, SWA_Q_W)
    w_out = w_out_c[0].astype(BF16)
    x1_p = _outproj_ln(x, attn_p, 0, attn_p, 1, w_out, b_out_c[0], ln_mix_g[1], ln_mix_b[1], ROW_TILE)
    x1_s = _outproj_ln(x[n_prompt:n_real], attn_s, 0, attn_s, 1, w_out, b_out_c[0], ln_mix_g[1],
                       ln_mix_b[1], n_dec)
    kv_p = _sequence_tails(proj, bsz, t_len, WINDOW)[:, :, SWA_Q_W:]
    heads = (SWA_KV_HEADS, SWA_HEAD_DIM)
    p_k = kv_p[:, :, :SWA_KV_W].reshape(bsz, WINDOW, *heads)
    p_v = kv_p[:, :, SWA_KV_W:].reshape(bsz, WINDOW, *heads)
    s_k = jnp.concatenate([kc[:, 1:], kn[:, None]], axis=1).reshape(n_dec, WINDOW, *heads)
    s_v = jnp.concatenate([vc[:, 1:], vn[:, None]], axis=1).reshape(n_dec, WINDOW, *heads)
    return x1_p, x1_s, p_k, p_v, s_k, s_v
```
